```python
import math
import jax, jax.numpy as jnp
from jax import lax
import numpy as np

D_MODEL = 1024
BATCH = 4
SEQ = 4096
DEPTH = 1

N_META = 16
BLOCK_Q = 128
DIFF_HEADS = 8
DIFF_HEAD_DIM = 64
DIFF_V_DIM = 2 * DIFF_HEAD_DIM
DIFF_QK = DIFF_HEADS * 2 * DIFF_HEAD_DIM
DIFF_WIDTH = DIFF_HEADS * DIFF_V_DIM
SB_HEADS = 16
SB_HEAD_DIM = 64
SB_WIDTH = SB_HEADS * SB_HEAD_DIM
IN_SIZES = [DIFF_QK, DIFF_QK, DIFF_WIDTH, DIFF_WIDTH,
            SB_WIDTH, SB_WIDTH, SB_WIDTH, SB_WIDTH,
            D_MODEL, D_MODEL]
IN_OFFSETS = [int(v) for v in np.cumsum(IN_SIZES)[:-1]]
N_IN = int(sum(IN_SIZES))
DN_ALPHA = (2.0 * DEPTH) ** 0.25
DN_BETA = (8.0 * DEPTH) ** -0.25
LN_EPS = 1e-5
RMS_EPS = 1e-5

kernel_name = 'hybrid_diffattn_stickbreaking_gated_merge'


def layer_norm(x, g, b):
    xf = x.astype(jnp.float32)
    mu = jnp.mean(xf, axis=-1, keepdims=True)
    var = jnp.mean(jnp.square(xf - mu), axis=-1, keepdims=True)
    return ((xf - mu) * lax.rsqrt(var + LN_EPS) * g.astype(jnp.float32) + b.astype(jnp.float32)).astype(x.dtype)


def rms_norm(x, g):
    xf = x.astype(jnp.float32)
    y = xf * lax.rsqrt(jnp.mean(jnp.square(xf), axis=-1, keepdims=True) + RMS_EPS)
    return (y * g.astype(jnp.float32)).astype(x.dtype)


def block_bounds(total_len):
    bounds = [(0, N_META)]
    for t0 in range(N_META, total_len, BLOCK_Q):
        bounds.append((t0, min(t0 + BLOCK_Q, total_len)))
    return bounds


def diff_attn_block(q, k, v, q_pos, slopes, lam):
    k_pos = jnp.arange(k.shape[3])
    dist = (q_pos[:, None] - k_pos[None, :])
    s = jnp.einsum('bhcqd,bhckd->bhcqk', q, k).astype(jnp.float32) * (DIFF_HEAD_DIM ** -0.5)
    alibi = -slopes[:, None, None] * dist.astype(jnp.float32)[None]
    s = jnp.where(dist >= 0, s + alibi[None, :, None], -jnp.inf)
    p = jax.nn.softmax(s, axis=-1)
    a = p[:, :, 0] - lam * p[:, :, 1]
    return jnp.einsum('bhqk,bhkd->bhqd', a.astype(v.dtype), v)


def stick_breaking_block(q, k, v, q_pos):
    k_pos = jnp.arange(k.shape[2])
    before = k_pos[None, :] < q_pos[:, None]
    z = jnp.einsum('bhqd,bhkd->bhqk', q, k).astype(jnp.float32) * (SB_HEAD_DIM ** -0.5)
    log_keep = jnp.where(before, jax.nn.log_sigmoid(-z), 0.0)
    suffix = lax.cumsum(log_keep, axis=3, reverse=True) - log_keep
    a = jnp.where(before, jnp.exp(jax.nn.log_sigmoid(z) + suffix), 0.0)
    return jnp.einsum('bhqk,bhkd->bhqd', a.astype(v.dtype), v)


def setup_inputs(seed: int = 0) -> dict:
    key = jax.random.key(seed)
    ks = jax.random.split(key, 14)
    f32 = jnp.float32
    x = jax.random.normal(ks[0], (BATCH, SEQ, D_MODEL), f32)
    meta_tokens = jax.random.normal(ks[1], (N_META, D_MODEL), f32)
    emb_ln_g = 1.0 + 0.02 * jax.random.normal(ks[2], (D_MODEL,), f32)
    emb_ln_b = 0.02 * jax.random.normal(ks[3], (D_MODEL,), f32)
    col_scale = np.concatenate([np.full((n,), DN_BETA if i in (2, 6) else 1.0, np.float32)
                                for i, n in enumerate(IN_SIZES)])
    w_in = jax.random.normal(ks[4], (DEPTH, D_MODEL, N_IN), f32) * (D_MODEL ** -0.5) * jnp.asarray(col_scale)
    b_gate = 0.01 * jax.random.normal(ks[5], (DEPTH, 2, D_MODEL), f32)
    diff_lambda = 0.1 * jax.random.normal(ks[6], (DEPTH, 4, DIFF_HEAD_DIM), f32)
    diff_subln_g = 1.0 + 0.02 * jax.random.normal(ks[7], (DEPTH, DIFF_V_DIM), f32)
    w_br_diff = jax.random.normal(ks[8], (DEPTH, DIFF_WIDTH, D_MODEL), f32) * (DIFF_WIDTH ** -0.5) * DN_BETA
    w_br_sb = jax.random.normal(ks[9], (DEPTH, SB_WIDTH, D_MODEL), f32) * (SB_WIDTH ** -0.5) * DN_BETA
    w_out = jax.random.normal(ks[10], (DEPTH, D_MODEL, D_MODEL), f32) * (D_MODEL ** -0.5) * DN_BETA
    ln_g = 1.0 + 0.02 * jax.random.normal(ks[11], (DEPTH, D_MODEL), f32)
    ln_b = 0.02 * jax.random.normal(ks[12], (DEPTH, D_MODEL), f32)
    return {'x': x, 'meta_tokens': meta_tokens, 'emb_ln_g': emb_ln_g, 'emb_ln_b': emb_ln_b,
            'w_in': w_in, 'b_gate': b_gate, 'diff_lambda': diff_lambda, 'diff_subln_g': diff_subln_g,
            'w_br_diff': w_br_diff, 'w_br_sb': w_br_sb, 'w_out': w_out, 'ln_g': ln_g, 'ln_b': ln_b}


def reference(x, meta_tokens, emb_ln_g, emb_ln_b, w_in, b_gate, diff_lambda, diff_subln_g,
              w_br_diff, w_br_sb, w_out, ln_g, ln_b):
    B = x.shape[0]
    meta = jnp.broadcast_to(meta_tokens.astype(x.dtype)[None], (B, N_META, D_MODEL))
    h = jnp.concatenate([meta, x], axis=1)
    L = h.shape[1]
    h = layer_norm(h, emb_ln_g, emb_ln_b)
    bounds = block_bounds(L)
    slopes = 2.0 ** (-8.0 * jnp.arange(1, DIFF_HEADS + 1, dtype=jnp.float32) / DIFF_HEADS)

    for i in range(DEPTH):
        lam_init = 0.8 - 0.6 * math.exp(-0.3 * i)
        lp = diff_lambda[i].astype(jnp.float32)
        lam = jnp.exp(jnp.sum(lp[0] * lp[1])) - jnp.exp(jnp.sum(lp[2] * lp[3])) + lam_init

        proj = jnp.einsum('bld,dn->bln', h, w_in[i])
        q_d, k_d, v_d, g_d, q_s, k_s, v_s, g_s, m_d, m_s = jnp.split(proj, IN_OFFSETS, axis=-1)
        q_d = q_d.reshape(B, L, DIFF_HEADS, 2, DIFF_HEAD_DIM).transpose(0, 2, 3, 1, 4)
        k_d = k_d.reshape(B, L, DIFF_HEADS, 2, DIFF_HEAD_DIM).transpose(0, 2, 3, 1, 4)
        v_d = v_d.reshape(B, L, DIFF_HEADS, DIFF_V_DIM).transpose(0, 2, 1, 3)
        q_s = q_s.reshape(B, L, SB_HEADS, SB_HEAD_DIM).transpose(0, 2, 1, 3)
        k_s = k_s.reshape(B, L, SB_HEADS, SB_HEAD_DIM).transpose(0, 2, 1, 3)
        v_s = v_s.reshape(B, L, SB_HEADS, SB_HEAD_DIM).transpose(0, 2, 1, 3)

        outs_d, outs_s = [], []
        for (t0, t1) in bounds:
            q_pos = jnp.arange(t0, t1)
            outs_d.append(diff_attn_block(q_d[:, :, :, t0:t1], k_d[:, :, :, :t1], v_d[:, :, :t1],
                                          q_pos, slopes, lam))
            outs_s.append(stick_breaking_block(q_s[:, :, t0:t1], k_s[:, :, :t1], v_s[:, :, :t1], q_pos))
        y_d = jnp.concatenate(outs_d, axis=2)
        y_s = jnp.concatenate(outs_s, axis=2)

        y_d = rms_norm(y_d, diff_subln_g[i]) * (1.0 - lam_init)
        y_d = y_d.transpose(0, 2, 1, 3).reshape(B, L, DIFF_WIDTH)
        y_s = y_s.transpose(0, 2, 1, 3).reshape(B, L, SB_WIDTH)

        br_d = jnp.einsum('blw,wd->bld', y_d * jax.nn.silu(g_d), w_br_diff[i])
        br_s = jnp.einsum('blw,wd->bld', y_s * jax.nn.silu(g_s), w_br_sb[i])
        merged = jax.nn.sigmoid(m_d + b_gate[i, 0]) * br_d + jax.nn.sigmoid(m_s + b_gate[i, 1]) * br_s
        y = jnp.einsum('bld,de->ble', merged, w_out[i])

        h = layer_norm(DN_ALPHA * h + y, ln_g[i], ln_b[i])

    return h[:, N_META:].astype(x.dtype)
```

```python
import functools
import math

import jax
import jax.numpy as jnp
from jax import lax
from jax.experimental import pallas as pl
from jax.experimental.pallas import tpu as pltpu

D_MODEL = 1024
N_META = 16
DIFF_HEADS = 8
DIFF_HEAD_DIM = 64
DIFF_V_DIM = 128
SB_HEADS = 16
SB_HEAD_DIM = 64
DEPTH = 1
DN_ALPHA = (2.0 * DEPTH) ** 0.25
LN_EPS = 1e-5
RMS_EPS = 1e-5
LAM_INIT = 0.8 - 0.6 * math.exp(-0.3 * 0)

LANES = 128
TQ = 256
TK = 256
TM_PROJ = 512
TM_FINAL = 256
VMEM_LIMIT = 56 * 1024 * 1024
LOG2E = 1.4426950408889634
NEG_BIG = -1e30
SB_EXIT_LOG2 = -104.0 * LOG2E - 1.0

_NT = (((1,), (1,)), ((), ()))


def _layer_norm(x, g, b):
    mu = jnp.mean(x, axis=-1, keepdims=True)
    xc = x - mu
    var = jnp.mean(xc * xc, axis=-1, keepdims=True)
    return xc * lax.rsqrt(var + LN_EPS) * g + b


def _proj_kernel(x_ref, g_ref, b_ref, w_ref, wvt_ref, qd_ref, kd_ref, qs_ref, ks_ref, vdt_ref, vst_ref):
    h = _layer_norm(x_ref[...], g_ref[...], b_ref[...]).astype(jnp.bfloat16)
    for c, o_ref in enumerate((qd_ref, kd_ref, qs_ref, ks_ref)):
        o = jnp.dot(h, w_ref[:, c * D_MODEL:(c + 1) * D_MODEL], preferred_element_type=jnp.float32)
        o_ref[...] = o.astype(jnp.bfloat16)
    for c, o_ref in enumerate((vdt_ref, vst_ref)):
        vt = lax.dot_general(wvt_ref[c * D_MODEL:(c + 1) * D_MODEL, :], h, _NT,
                             preferred_element_type=jnp.float32).astype(jnp.bfloat16)
        for t in range(TM_PROJ // TK):
            o_ref[t] = vt[:, t * TK:(t + 1) * TK]


def _proj_call(x, g, b, w_qk, w_vt):
    B, S, D = x.shape
    nt = S // TM_PROJ
    row_spec = pl.BlockSpec((None, TM_PROJ, D), lambda bi, i: (bi, i, 0))
    vt_spec = pl.BlockSpec((None, TM_PROJ // TK, D, TK), lambda bi, i: (bi, i, 0, 0))
    const = lambda bi, i: (0, 0)
    return pl.pallas_call(
        _proj_kernel,
        grid=(B, nt),
        in_specs=[row_spec,
                  pl.BlockSpec((1, D), const), pl.BlockSpec((1, D), const),
                  pl.BlockSpec((D, 4 * D), const), pl.BlockSpec((2 * D, D), const)],
        out_specs=[row_spec, row_spec, row_spec, row_spec, vt_spec, vt_spec],
        out_shape=[jax.ShapeDtypeStruct((B, S, D), jnp.bfloat16)] * 4
        + [jax.ShapeDtypeStruct((B, S // TK, D, TK), jnp.bfloat16)] * 2,
        compiler_params=pltpu.CompilerParams(dimension_semantics=("arbitrary", "arbitrary"),
                                             vmem_limit_bytes=VMEM_LIMIT),
        name="proj",
    )(x, g, b, w_qk, w_vt)


def _meta_proj_kernel(m_ref, g_ref, b_ref, w_ref, wvt_ref, k_ref, vt_ref):
    h = _layer_norm(m_ref[...], g_ref[...], b_ref[...]).astype(jnp.bfloat16)
    k_ref[...] = jnp.dot(h, w_ref[...], preferred_element_type=jnp.float32).astype(jnp.bfloat16)
    vt_ref[...] = lax.dot_general(wvt_ref[...], h, _NT,
                                  preferred_element_type=jnp.float32).astype(jnp.bfloat16)


def _meta_proj_call(meta, g, b, w_k, w_vt):
    D = D_MODEL
    return pl.pallas_call(
        _meta_proj_kernel,
        out_shape=[jax.ShapeDtypeStruct((N_META, 2 * D), jnp.bfloat16),
                   jax.ShapeDtypeStruct((2 * D, N_META), jnp.bfloat16)],
        compiler_params=pltpu.CompilerParams(vmem_limit_bytes=VMEM_LIMIT),
        name="meta_proj",
    )(meta, g, b, w_k, w_vt)


def _diff_kernel(slope_ref, lam_ref, q_ref, k_ref, vt_ref, km_ref, vmt_ref, g_ref, o_ref):
    hd = pl.program_id(1)
    qi = pl.program_id(2)
    slope = slope_ref[hd]
    lam = (jnp.exp(jnp.sum(lam_ref[0:1, :] * lam_ref[1:2, :], axis=1, keepdims=True))
           - jnp.exp(jnp.sum(lam_ref[2:3, :] * lam_ref[3:4, :], axis=1, keepdims=True)) + LAM_INIT)

    q = q_ref[...]
    lane = lax.broadcasted_iota(jnp.int32, q.shape, 1)
    zero = jnp.zeros_like(q)
    qc = (jnp.where(lane < DIFF_HEAD_DIM, q, zero), jnp.where(lane >= DIFF_HEAD_DIM, q, zero))

    row = lax.broadcasted_iota(jnp.int32, (TK, TQ), 0)
    col = lax.broadcasted_iota(jnp.int32, (TK, TQ), 1)
    bias2 = (slope * LOG2E) * row.astype(jnp.float32)

    def scores2(k_tile, c):
        s = lax.dot_general(k_tile, qc[c], _NT, preferred_element_type=jnp.float32)
        return s * LOG2E

    def step(k_tile, vt_tile, bias, off2, mask, state):
        new = []
        for c in range(2):
            m, l, acc = state[c]
            s = scores2(k_tile, c) + bias
            if mask is not None:
                s = jnp.where(mask, s, NEG_BIG)
            m_new = jnp.maximum(m, jnp.max(s, axis=0, keepdims=True) + off2)
            alpha = jnp.exp2(m - m_new)
            p = jnp.exp2(s - (m_new - off2))
            l = alpha * l + jnp.sum(p, axis=0, keepdims=True)
            acc = alpha * acc + jnp.dot(vt_tile, p.astype(jnp.bfloat16), preferred_element_type=jnp.float32)
            new.append((m_new, l, acc))
        return tuple(new)

    init = tuple((jnp.full((1, TQ), NEG_BIG, jnp.float32), jnp.zeros((1, TQ), jnp.float32),
                  jnp.zeros((DIFF_V_DIM, TQ), jnp.float32)) for _ in range(2))

    k0 = pl.multiple_of(qi * TQ, TQ)
    state = step(k_ref[pl.ds(k0, TK), :], vt_ref[qi], bias2, 0.0, row <= col, init)

    def body(i, st):
        j = qi - 1 - i
        off2 = -(slope * LOG2E * TK) * (i + 1).astype(jnp.float32)
        kk = pl.multiple_of(j * TK, TK)
        return step(k_ref[pl.ds(kk, TK), :], vt_ref[j], bias2, off2, None, st)

    state = lax.fori_loop(0, qi, body, state)

    off_m = -(slope * LOG2E) * (qi * TQ + N_META).astype(jnp.float32)
    bias_m = (slope * LOG2E) * lax.broadcasted_iota(jnp.int32, (N_META, TQ), 0).astype(jnp.float32)
    state = step(km_ref[...], vmt_ref[...], bias_m, off_m, None, state)

    (_, l1, a1), (_, l2, a2) = state
    y = a1 * (1.0 / l1) - lam * (a2 * (1.0 / l2))
    ms = jnp.mean(y * y, axis=0, keepdims=True)
    y = y * lax.rsqrt(ms + RMS_EPS) * g_ref[...] * (1.0 - LAM_INIT)
    o_ref[...] = y.T.astype(o_ref.dtype)


def _diff_call(slopes, lam, qd, kd, vdt, kdm, vdmt, g_col):
    B, S, D = qd.shape
    nq = S // TQ
    smem = pl.BlockSpec(memory_space=pltpu.SMEM)
    return pl.pallas_call(
        _diff_kernel,
        grid=(B, DIFF_HEADS, nq),
        in_specs=[smem, pl.BlockSpec((4, DIFF_HEAD_DIM), lambda b, h, i: (0, 0)),
                  pl.BlockSpec((None, TQ, LANES), lambda b, h, i: (b, i, h)),
                  pl.BlockSpec((None, S, LANES), lambda b, h, i: (b, 0, h)),
                  pl.BlockSpec((None, S // TK, DIFF_V_DIM, TK), lambda b, h, i: (b, 0, h, 0)),
                  pl.BlockSpec((N_META, LANES), lambda b, h, i: (0, h)),
                  pl.BlockSpec((DIFF_V_DIM, N_META), lambda b, h, i: (h, 0)),
                  pl.BlockSpec((DIFF_V_DIM, 1), lambda b, h, i: (0, 0))],
        out_specs=pl.BlockSpec((None, TQ, LANES), lambda b, h, i: (b, i, h)),
        out_shape=jax.ShapeDtypeStruct((B, S, D), jnp.bfloat16),
        compiler_params=pltpu.CompilerParams(dimension_semantics=("arbitrary",) * 3,
                                             vmem_limit_bytes=VMEM_LIMIT),
        name="diff_attn",
    )(slopes, lam, qd, kd, vdt, kdm, vdmt, g_col)


def _sb_kernel(q_ref, k_ref, vt_ref, km_ref, vmt_ref, u_ref, um_ref, o_ref):
    qi = pl.program_id(2)
    q = q_ref[...]
    lane = lax.broadcasted_iota(jnp.int32, q.shape, 1)
    zero = jnp.zeros_like(q)
    row = lax.broadcasted_iota(jnp.int32, (TK, TQ), 0)
    col = lax.broadcasted_iota(jnp.int32, (TK, TQ), 1)
    strict = row < col

    def tile(k_tile, v_tile, u, mask, qh, carry, acc):
        z2 = lax.dot_general(k_tile, qh, _NT, preferred_element_type=jnp.float32) * LOG2E
        e = jnp.exp2(-jnp.abs(z2))
        sp2 = jnp.maximum(z2, 0.0) + jnp.log2(1.0 + e)
        if mask is not None:
            sp2 = jnp.where(mask, sp2, 0.0)
        hi = sp2.astype(jnp.bfloat16)
        lo = (sp2 - hi.astype(jnp.float32)).astype(jnp.bfloat16)
        incl = (jnp.dot(u, hi, preferred_element_type=jnp.float32)
                + jnp.dot(u, lo, preferred_element_type=jnp.float32))
        a = jnp.exp2(z2 + incl + carry)
        if mask is not None:
            a = jnp.where(mask, a, 0.0)
        acc = acc + jnp.dot(v_tile, a.astype(jnp.bfloat16), preferred_element_type=jnp.float32)
        return carry + incl[0:1, :], acc

    outs = []
    for hh in range(2):
        sel = (lane < SB_HEAD_DIM) if hh == 0 else (lane >= SB_HEAD_DIM)
        qh = jnp.where(sel, q, zero)
        r0 = hh * SB_HEAD_DIM

        k0 = pl.multiple_of(qi * TQ, TQ)
        carry, acc = tile(k_ref[pl.ds(k0, TK), :], vt_ref[qi, r0:r0 + SB_HEAD_DIM, :], u_ref[...], strict, qh,
                          jnp.zeros((1, TQ), jnp.float32), jnp.zeros((SB_HEAD_DIM, TQ), jnp.float32))

        def cond(st):
            j, cmax, _, _ = st
            return jnp.logical_and(j >= 0, cmax > SB_EXIT_LOG2)

        def body(st):
            j, _, carry, acc = st
            kk = pl.multiple_of(j * TK, TK)
            carry, acc = tile(k_ref[pl.ds(kk, TK), :], vt_ref[j, r0:r0 + SB_HEAD_DIM, :], u_ref[...], None, qh,
                              carry, acc)
            return j - 1, jnp.max(carry), carry, acc

        _, cmax, carry, acc = lax.while_loop(cond, body, (qi - 1, jnp.max(carry), carry, acc))

        def meta(acc):
            return tile(km_ref[...], vmt_ref[r0:r0 + SB_HEAD_DIM, :], um_ref[...], None, qh, carry, acc)[1]

        acc = lax.cond(cmax > SB_EXIT_LOG2, meta, lambda a: a, acc)
        outs.append(acc)

    o_ref[...] = jnp.concatenate(outs, axis=0).T.astype(o_ref.dtype)


def _sb_call(qs, ks, vst, ksm, vsmt, u, um):
    B, S, D = qs.shape
    nq = S // TQ
    const = lambda b, h, i: (0, 0)
    return pl.pallas_call(
        _sb_kernel,
        grid=(B, SB_HEADS // 2, nq),
        in_specs=[pl.BlockSpec((None, TQ, LANES), lambda b, h, i: (b, i, h)),
                  pl.BlockSpec((None, S, LANES), lambda b, h, i: (b, 0, h)),
                  pl.BlockSpec((None, S // TK, LANES, TK), lambda b, h, i: (b, 0, h, 0)),
                  pl.BlockSpec((N_META, LANES), lambda b, h, i: (0, h)),
                  pl.BlockSpec((LANES, N_META), lambda b, h, i: (h, 0)),
                  pl.BlockSpec((TK, TK), const), pl.BlockSpec((N_META, N_META), const)],
        out_specs=pl.BlockSpec((None, TQ, LANES), lambda b, h, i: (b, i, h)),
        out_shape=jax.ShapeDtypeStruct((B, S, D), jnp.bfloat16),
        compiler_params=pltpu.CompilerParams(dimension_semantics=("arbitrary",) * 3,
                                             vmem_limit_bytes=VMEM_LIMIT),
        name="sb_attn",
    )(qs, ks, vst, ksm, vsmt, u, um)


def _sigmoid(x):
    return 1.0 / (1.0 + jnp.exp(-x))


def _final_kernel(x_ref, yd_ref, ys_ref, eg_ref, eb_ref, wg_ref, wbd_ref, wbs_ref, wo_ref, bg_ref,
                  lg_ref, lb_ref, o_ref):
    D = D_MODEL
    h = _layer_norm(x_ref[...], eg_ref[...], eb_ref[...])
    hb = h.astype(jnp.bfloat16)

    def gate(c):
        return jnp.dot(hb, wg_ref[:, c * D:(c + 1) * D], preferred_element_type=jnp.float32)

    g_d = gate(0)
    u_d = (yd_ref[...].astype(jnp.float32) * (g_d * _sigmoid(g_d))).astype(jnp.bfloat16)
    br_d = jnp.dot(u_d, wbd_ref[...], preferred_element_type=jnp.float32)
    merged = _sigmoid(gate(2) + bg_ref[0:1, :]) * br_d
    g_s = gate(1)
    u_s = (ys_ref[...].astype(jnp.float32) * (g_s * _sigmoid(g_s))).astype(jnp.bfloat16)
    br_s = jnp.dot(u_s, wbs_ref[...], preferred_element_type=jnp.float32)
    merged = merged + _sigmoid(gate(3) + bg_ref[1:2, :]) * br_s
    y = jnp.dot(merged.astype(jnp.bfloat16), wo_ref[...], preferred_element_type=jnp.float32)
    o_ref[...] = _layer_norm(DN_ALPHA * h + y, lg_ref[...], lb_ref[...]).astype(o_ref.dtype)


def _final_call(x2, yd2, ys2, eg, eb, wg, wbd, wbs, wo, bg, lg, lb):
    R, D = x2.shape
    row = pl.BlockSpec((TM_FINAL, D), lambda i: (i, 0))
    const = lambda i: (0, 0)
    vec = pl.BlockSpec((1, D), const)
    return pl.pallas_call(
        _final_kernel,
        grid=(R // TM_FINAL,),
        in_specs=[row, row, row, vec, vec,
                  pl.BlockSpec((D, 4 * D), const), pl.BlockSpec((D, D), const),
                  pl.BlockSpec((D, D), const), pl.BlockSpec((D, D), const),
                  pl.BlockSpec((2, D), const), vec, vec],
        out_specs=row,
        out_shape=jax.ShapeDtypeStruct((R, D), jnp.float32),
        compiler_params=pltpu.CompilerParams(dimension_semantics=("arbitrary",),
                                             vmem_limit_bytes=VMEM_LIMIT),
        name="final",
    )(x2, yd2, ys2, eg, eb, wg, wbd, wbs, wo, bg, lg, lb)


def _neg_upper_ones(n):
    r = jnp.arange(n)
    return jnp.where(r[None, :] >= r[:, None], -1.0, 0.0).astype(jnp.bfloat16)


def kernel(x, meta_tokens, emb_ln_g, emb_ln_b, w_in, b_gate, diff_lambda, diff_subln_g,
           w_br_diff, w_br_sb, w_out, ln_g, ln_b):
    B, S, D = x.shape
    bf16 = jnp.bfloat16
    w = w_in[0]
    col = lambda i: w[:, i * D:(i + 1) * D]
    qscale = DIFF_HEAD_DIM ** -0.5
    w_qk = jnp.concatenate([col(0) * qscale, col(1), col(4) * qscale, col(5)], axis=1).astype(bf16)
    w_vt = jnp.concatenate([col(2).T, col(6).T], axis=0).astype(bf16)
    w_kmeta = jnp.concatenate([col(1), col(5)], axis=1).astype(bf16)
    w_gate = jnp.concatenate([col(3), col(7), col(8), col(9)], axis=1).astype(bf16)
    eg, eb = emb_ln_g.reshape(1, D), emb_ln_b.reshape(1, D)

    qd, kd, qs, ks, vdt, vst = _proj_call(x, eg, eb, w_qk, w_vt)
    k_meta, vt_meta = _meta_proj_call(meta_tokens, eg, eb, w_kmeta, w_vt)

    slopes = 2.0 ** (-8.0 * jnp.arange(1, DIFF_HEADS + 1, dtype=jnp.float32) / DIFF_HEADS)

    yd = _diff_call(slopes, diff_lambda[0], qd, kd, vdt, k_meta[:, :D], vt_meta[:D], diff_subln_g[0].reshape(DIFF_V_DIM, 1))
    ys = _sb_call(qs, ks, vst, k_meta[:, D:], vt_meta[D:], _neg_upper_ones(TK), _neg_upper_ones(N_META))

    out = _final_call(x.reshape(B * S, D), yd.reshape(B * S, D), ys.reshape(B * S, D), eg, eb,
                      w_gate, w_br_diff[0].astype(bf16), w_br_sb[0].astype(bf16), w_out[0].astype(bf16),
                      b_gate[0], ln_g[0].reshape(1, D), ln_b[0].reshape(1, D))
    return out.reshape(B, S, D)
```

```python
import math

import jax
import jax.numpy as jnp
from jax import lax
from jax.experimental import pallas as pl
from jax.experimental.pallas import tpu as pltpu

D_MODEL = 1024
N_META = 16
DIFF_HEADS = 8
DIFF_HEAD_DIM = 64
DIFF_V_DIM = 128
SB_HEADS = 16
SB_HEAD_DIM = 64
DEPTH = 1
DN_ALPHA = (2.0 * DEPTH) ** 0.25
LN_EPS = 1e-5
RMS_EPS = 1e-5
LAM_INIT = 0.8 - 0.6 * math.exp(-0.3 * 0)

LANES = 128
TD = 512
TS = 256
TM_PROJ = 512
TM_FINAL = 256
VMEM_LIMIT = 56 * 1024 * 1024
LOG2E = 1.4426950408889634
NEG_BIG = -1e30
SB_EXIT_LOG = -105.0

_NT = (((1,), (1,)), ((), ()))


def _layer_norm(x, g, b):
    mu = jnp.mean(x, axis=-1, keepdims=True)
    xc = x - mu
    var = jnp.mean(xc * xc, axis=-1, keepdims=True)
    return xc * lax.rsqrt(var + LN_EPS) * g + b


def _proj_kernel(x_ref, g_ref, b_ref, w_ref, wvt_ref, qd_ref, kd_ref, qs_ref, ks_ref, vdt_ref, vst_ref):
    h = _layer_norm(x_ref[...], g_ref[...], b_ref[...]).astype(jnp.bfloat16)
    for c, o_ref in enumerate((qd_ref, kd_ref, qs_ref, ks_ref)):
        o = jnp.dot(h, w_ref[:, c * D_MODEL:(c + 1) * D_MODEL], preferred_element_type=jnp.float32)
        o_ref[...] = o.astype(jnp.bfloat16)
    for c, (o_ref, tk) in enumerate(((vdt_ref, TD), (vst_ref, TS))):
        vt = lax.dot_general(wvt_ref[c * D_MODEL:(c + 1) * D_MODEL, :], h, _NT,
                             preferred_element_type=jnp.float32).astype(jnp.bfloat16)
        for t in range(TM_PROJ // tk):
            o_ref[t] = vt[:, t * tk:(t + 1) * tk]


def _proj_call(x, g, b, w_qk, w_vt):
    B, S, D = x.shape
    nt = S // TM_PROJ
    row_spec = pl.BlockSpec((None, TM_PROJ, D), lambda bi, i: (bi, i, 0))
    vt_specs = [pl.BlockSpec((None, TM_PROJ // tk, D, tk), lambda bi, i: (bi, i, 0, 0)) for tk in (TD, TS)]
    const = lambda bi, i: (0, 0)
    return pl.pallas_call(
        _proj_kernel,
        grid=(B, nt),
        in_specs=[row_spec,
                  pl.BlockSpec((1, D), const), pl.BlockSpec((1, D), const),
                  pl.BlockSpec((D, 4 * D), const), pl.BlockSpec((2 * D, D), const)],
        out_specs=[row_spec, row_spec, row_spec, row_spec] + vt_specs,
        out_shape=[jax.ShapeDtypeStruct((B, S, D), jnp.bfloat16)] * 4
        + [jax.ShapeDtypeStruct((B, S // tk, D, tk), jnp.bfloat16) for tk in (TD, TS)],
        compiler_params=pltpu.CompilerParams(dimension_semantics=("arbitrary", "arbitrary"),
                                             vmem_limit_bytes=VMEM_LIMIT),
        name="proj",
    )(x, g, b, w_qk, w_vt)


def _meta_proj_kernel(m_ref, g_ref, b_ref, w_ref, wvt_ref, k_ref, vt_ref):
    h = _layer_norm(m_ref[...], g_ref[...], b_ref[...]).astype(jnp.bfloat16)
    k_ref[...] = jnp.dot(h, w_ref[...], preferred_element_type=jnp.float32).astype(jnp.bfloat16)
    vt_ref[...] = lax.dot_general(wvt_ref[...], h, _NT,
                                  preferred_element_type=jnp.float32).astype(jnp.bfloat16)


def _meta_proj_call(meta, g, b, w_k, w_vt):
    D = D_MODEL
    return pl.pallas_call(
        _meta_proj_kernel,
        out_shape=[jax.ShapeDtypeStruct((N_META, 2 * D), jnp.bfloat16),
                   jax.ShapeDtypeStruct((2 * D, N_META), jnp.bfloat16)],
        compiler_params=pltpu.CompilerParams(vmem_limit_bytes=VMEM_LIMIT),
        name="meta_proj",
    )(meta, g, b, w_k, w_vt)


def _diff_kernel(slope_ref, lam_ref, q_ref, k_ref, vt_ref, km_ref, vmt_ref, g_ref, o_ref):
    hd = pl.program_id(1)
    qi = pl.program_id(2)
    slope = slope_ref[hd]
    lam = (jnp.exp(jnp.sum(lam_ref[0:1, :] * lam_ref[1:2, :], axis=1, keepdims=True))
           - jnp.exp(jnp.sum(lam_ref[2:3, :] * lam_ref[3:4, :], axis=1, keepdims=True)) + LAM_INIT)

    q = q_ref[...]
    lane = lax.broadcasted_iota(jnp.int32, q.shape, 1)
    zero = jnp.zeros_like(q)
    qc = (jnp.where(lane < DIFF_HEAD_DIM, q, zero), jnp.where(lane >= DIFF_HEAD_DIM, q, zero))

    row = lax.broadcasted_iota(jnp.int32, (TD, TD), 0)
    col = lax.broadcasted_iota(jnp.int32, (TD, TD), 1)
    bias2 = (slope * LOG2E) * row.astype(jnp.float32)

    def scores2(k_tile, c):
        s = lax.dot_general(k_tile, qc[c], _NT, preferred_element_type=jnp.float32)
        return s * LOG2E

    def step(k_tile, vt_tile, bias, off2, mask, state):
        new = []
        for c in range(2):
            m, l, acc = state[c]
            s = scores2(k_tile, c) + bias
            if mask is not None:
                s = jnp.where(mask, s, NEG_BIG)
            m_new = jnp.maximum(m, jnp.max(s, axis=0, keepdims=True) + off2)
            alpha = jnp.exp2(m - m_new)
            p = jnp.exp2(s - (m_new - off2))
            l = alpha * l + jnp.sum(p, axis=0, keepdims=True)
            acc = alpha * acc + jnp.dot(vt_tile, p.astype(jnp.bfloat16), preferred_element_type=jnp.float32)
            new.append((m_new, l, acc))
        return tuple(new)

    init = tuple((jnp.full((1, TD), NEG_BIG, jnp.float32), jnp.zeros((1, TD), jnp.float32),
                  jnp.zeros((DIFF_V_DIM, TD), jnp.float32)) for _ in range(2))

    k0 = pl.multiple_of(qi * TD, TD)
    state = step(k_ref[pl.ds(k0, TD), :], vt_ref[qi], bias2, 0.0, row <= col, init)

    def body(i, st):
        j = qi - 1 - i
        off2 = -(slope * LOG2E * TD) * (i + 1).astype(jnp.float32)
        kk = pl.multiple_of(j * TD, TD)
        return step(k_ref[pl.ds(kk, TD), :], vt_ref[j], bias2, off2, None, st)

    state = lax.fori_loop(0, qi, body, state)

    off_m = -(slope * LOG2E) * (qi * TD + N_META).astype(jnp.float32)
    bias_m = (slope * LOG2E) * lax.broadcasted_iota(jnp.int32, (N_META, TD), 0).astype(jnp.float32)
    state = step(km_ref[...], vmt_ref[...], bias_m, off_m, None, state)

    (_, l1, a1), (_, l2, a2) = state
    y = a1 * (1.0 / l1) - lam * (a2 * (1.0 / l2))
    ms = jnp.mean(y * y, axis=0, keepdims=True)
    y = y * lax.rsqrt(ms + RMS_EPS) * g_ref[...] * (1.0 - LAM_INIT)
    o_ref[...] = y.T.astype(o_ref.dtype)


def _diff_call(slopes, lam, qd, kd, vdt, kdm, vdmt, g_col):
    B, S, D = qd.shape
    nq = S // TD
    smem = pl.BlockSpec(memory_space=pltpu.SMEM)
    return pl.pallas_call(
        _diff_kernel,
        grid=(B, DIFF_HEADS, nq),
        in_specs=[smem, pl.BlockSpec((4, DIFF_HEAD_DIM), lambda b, h, i: (0, 0)),
                  pl.BlockSpec((None, TD, LANES), lambda b, h, i: (b, i, h)),
                  pl.BlockSpec((None, S, LANES), lambda b, h, i: (b, 0, h)),
                  pl.BlockSpec((None, S // TD, DIFF_V_DIM, TD), lambda b, h, i: (b, 0, h, 0)),
                  pl.BlockSpec((N_META, LANES), lambda b, h, i: (0, h)),
                  pl.BlockSpec((DIFF_V_DIM, N_META), lambda b, h, i: (h, 0)),
                  pl.BlockSpec((DIFF_V_DIM, 1), lambda b, h, i: (0, 0))],
        out_specs=pl.BlockSpec((None, TD, LANES), lambda b, h, i: (b, i, h)),
        out_shape=jax.ShapeDtypeStruct((B, S, D), jnp.bfloat16),
        compiler_params=pltpu.CompilerParams(dimension_semantics=("arbitrary",) * 3,
                                             vmem_limit_bytes=VMEM_LIMIT),
        name="diff_attn",
    )(slopes, lam, qd, kd, vdt, kdm, vdmt, g_col)


def _sb_kernel(q_ref, k_ref, vt_ref, km_ref, vmt_ref, u_ref, um_ref, o_ref):
    qi = pl.program_id(2)
    q = q_ref[...]
    lane = lax.broadcasted_iota(jnp.int32, q.shape, 1)
    zero = jnp.zeros_like(q)
    row = lax.broadcasted_iota(jnp.int32, (TS, TS), 0)
    col = lax.broadcasted_iota(jnp.int32, (TS, TS), 1)
    strict = row < col

    def logits(k_tile, qh):
        return lax.dot_general(k_tile, qh, _NT, preferred_element_type=jnp.float32)

    def split_softplus(z, mask):
        e = jnp.exp2(jnp.abs(z) * (-LOG2E))
        sp = jnp.maximum(z, 0.0) + jnp.log(1.0 + e)
        if mask is not None:
            sp = jnp.where(mask, sp, 0.0)
        hi = sp.astype(jnp.bfloat16)
        return hi, (sp - hi.astype(jnp.float32)).astype(jnp.bfloat16)

    def suffix(u, hi, lo):
        return jnp.dot(u, hi, preferred_element_type=jnp.float32) + jnp.dot(u, lo, preferred_element_type=jnp.float32)

    def weights(z, incl, carry, mask):
        a = jnp.exp(z + incl + carry)
        if mask is not None:
            a = jnp.where(mask, a, 0.0)
        return a.astype(jnp.bfloat16)

    def tile(k_tile, v_tile, u, qh, carry, acc):
        z = logits(k_tile, qh)
        incl = suffix(u, *split_softplus(z, None))
        pv = jnp.dot(v_tile, weights(z, incl, carry, None), preferred_element_type=jnp.float32)
        return carry + incl[0:1, :], acc + pv

    has_left = qi > 0
    jl = jnp.maximum(qi - 1, 0)
    k0 = pl.multiple_of(qi * TS, TS)
    kl = pl.multiple_of(jl * TS, TS)
    u = u_ref[...]
    qhs = [jnp.where((lane < SB_HEAD_DIM) if hh == 0 else (lane >= SB_HEAD_DIM), q, zero) for hh in range(2)]

    chains = [(hh, strict, k0, qi) for hh in range(2)] + [(hh, None, kl, jl) for hh in range(2)]
    zs = [logits(k_ref[pl.ds(kk, TS), :], qhs[hh]) for hh, _, kk, _ in chains]
    parts = [split_softplus(z, mask) for z, (_, mask, _, _) in zip(zs, chains)]
    incls = [suffix(u, hi, lo) for hi, lo in parts]
    tots = [incl[0:1, :] for incl in incls]
    carries = [0.0, 0.0] + [jnp.where(has_left, tots[hh], NEG_BIG) for hh in range(2)]
    ws = [weights(z, incl, c, mask) for z, incl, c, (_, mask, _, _) in zip(zs, incls, carries, chains)]
    pvs = [jnp.dot(vt_ref[j, hh * SB_HEAD_DIM:(hh + 1) * SB_HEAD_DIM, :], w, preferred_element_type=jnp.float32)
           for w, (hh, _, _, j) in zip(ws, chains)]
    near = [(qhs[hh], hh * SB_HEAD_DIM, tots[hh] + jnp.where(has_left, tots[2 + hh], 0.0), pvs[hh] + pvs[2 + hh])
            for hh in range(2)]

    outs = []
    for qh, r0, carry, acc in near:

        def cond(st):
            j, cmax, _, _ = st
            return jnp.logical_and(j >= 0, cmax > SB_EXIT_LOG)

        def body(st):
            j, _, carry, acc = st
            kk = pl.multiple_of(j * TS, TS)
            carry, acc = tile(k_ref[pl.ds(kk, TS), :], vt_ref[j, r0:r0 + SB_HEAD_DIM, :], u, qh, carry, acc)
            return j - 1, jnp.max(carry), carry, acc

        _, cmax, carry, acc = lax.while_loop(cond, body, (qi - 2, jnp.max(carry), carry, acc))

        def meta(acc):
            return tile(km_ref[...], vmt_ref[r0:r0 + SB_HEAD_DIM, :], um_ref[...], qh, carry, acc)[1]

        acc = lax.cond(cmax > SB_EXIT_LOG, meta, lambda a: a, acc)
        outs.append(acc)

    o_ref[...] = jnp.concatenate(outs, axis=0).T.astype(o_ref.dtype)


def _sb_call(qs, ks, vst, ksm, vsmt, u, um):
    B, S, D = qs.shape
    nq = S // TS
    const = lambda b, h, i: (0, 0)
    return pl.pallas_call(
        _sb_kernel,
        grid=(B, SB_HEADS // 2, nq),
        in_specs=[pl.BlockSpec((None, TS, LANES), lambda b, h, i: (b, i, h)),
                  pl.BlockSpec((None, S, LANES), lambda b, h, i: (b, 0, h)),
                  pl.BlockSpec((None, S // TS, LANES, TS), lambda b, h, i: (b, 0, h, 0)),
                  pl.BlockSpec((N_META, LANES), lambda b, h, i: (0, h)),
                  pl.BlockSpec((LANES, N_META), lambda b, h, i: (h, 0)),
                  pl.BlockSpec((TS, TS), const), pl.BlockSpec((N_META, N_META), const)],
        out_specs=pl.BlockSpec((None, TS, LANES), lambda b, h, i: (b, i, h)),
        out_shape=jax.ShapeDtypeStruct((B, S, D), jnp.bfloat16),
        compiler_params=pltpu.CompilerParams(dimension_semantics=("arbitrary",) * 3,
                                             vmem_limit_bytes=VMEM_LIMIT),
        name="sb_attn",
    )(qs, ks, vst, ksm, vsmt, u, um)


def _sigmoid(x):
    return 1.0 / (1.0 + jnp.exp(-x))


def _final_kernel(x_ref, yd_ref, ys_ref, eg_ref, eb_ref, wg_ref, wbd_ref, wbs_ref, wo_ref, bg_ref,
                  lg_ref, lb_ref, o_ref):
    D = D_MODEL
    h = _layer_norm(x_ref[...], eg_ref[...], eb_ref[...])
    hb = h.astype(jnp.bfloat16)

    def gate(c):
        return jnp.dot(hb, wg_ref[:, c * D:(c + 1) * D], preferred_element_type=jnp.float32)

    g_d = gate(0)
    u_d = (yd_ref[...].astype(jnp.float32) * (g_d * _sigmoid(g_d))).astype(jnp.bfloat16)
    br_d = jnp.dot(u_d, wbd_ref[...], preferred_element_type=jnp.float32)
    merged = _sigmoid(gate(2) + bg_ref[0:1, :]) * br_d
    g_s = gate(1)
    u_s = (ys_ref[...].astype(jnp.float32) * (g_s * _sigmoid(g_s))).astype(jnp.bfloat16)
    br_s = jnp.dot(u_s, wbs_ref[...], preferred_element_type=jnp.float32)
    merged = merged + _sigmoid(gate(3) + bg_ref[1:2, :]) * br_s
    y = jnp.dot(merged.astype(jnp.bfloat16), wo_ref[...], preferred_element_type=jnp.float32)
    o_ref[...] = _layer_norm(DN_ALPHA * h + y, lg_ref[...], lb_ref[...]).astype(o_ref.dtype)


def _final_call(x2, yd2, ys2, eg, eb, wg, wbd, wbs, wo, bg, lg, lb):
    R, D = x2.shape
    row = pl.BlockSpec((TM_FINAL, D), lambda i: (i, 0))
    const = lambda i: (0, 0)
    vec = pl.BlockSpec((1, D), const)
    return pl.pallas_call(
        _final_kernel,
        grid=(R // TM_FINAL,),
        in_specs=[row, row, row, vec, vec,
                  pl.BlockSpec((D, 4 * D), const), pl.BlockSpec((D, D), const),
                  pl.BlockSpec((D, D), const), pl.BlockSpec((D, D), const),
                  pl.BlockSpec((2, D), const), vec, vec],
        out_specs=row,
        out_shape=jax.ShapeDtypeStruct((R, D), jnp.float32),
        compiler_params=pltpu.CompilerParams(dimension_semantics=("arbitrary",),
                                             vmem_limit_bytes=VMEM_LIMIT),
        name="final",
    )(x2, yd2, ys2, eg, eb, wg, wbd, wbs, wo, bg, lg, lb)


def _neg_upper_ones(n):
    r = jnp.arange(n)
    return jnp.where(r[None, :] >= r[:, None], -1.0, 0.0).astype(jnp.bfloat16)


def kernel(x, meta_tokens, emb_ln_g, emb_ln_b, w_in, b_gate, diff_lambda, diff_subln_g,
           w_br_diff, w_br_sb, w_out, ln_g, ln_b):
    B, S, D = x.shape
    bf16 = jnp.bfloat16
    w = w_in[0]
    col = lambda i: w[:, i * D:(i + 1) * D]
    qscale = DIFF_HEAD_DIM ** -0.5
    w_qk = jnp.concatenate([col(0) * qscale, col(1), col(4) * qscale, col(5)], axis=1).astype(bf16)
    w_vt = jnp.concatenate([col(2).T, col(6).T], axis=0).astype(bf16)
    w_kmeta = jnp.concatenate([col(1), col(5)], axis=1).astype(bf16)
    w_gate = jnp.concatenate([col(3), col(7), col(8), col(9)], axis=1).astype(bf16)
    eg, eb = emb_ln_g.reshape(1, D), emb_ln_b.reshape(1, D)

    qd, kd, qs, ks, vdt, vst = _proj_call(x, eg, eb, w_qk, w_vt)
    k_meta, vt_meta = _meta_proj_call(meta_tokens, eg, eb, w_kmeta, w_vt)

    slopes = 2.0 ** (-8.0 * jnp.arange(1, DIFF_HEADS + 1, dtype=jnp.float32) / DIFF_HEADS)

    yd = _diff_call(slopes, diff_lambda[0], qd, kd, vdt, k_meta[:, :D], vt_meta[:D], diff_subln_g[0].reshape(DIFF_V_DIM, 1))
    ys = _sb_call(qs, ks, vst, k_meta[:, D:], vt_meta[D:], _neg_upper_ones(TS), _neg_upper_ones(N_META))

    out = _final_call(x.reshape(B * S, D), yd.reshape(B * S, D), ys.reshape(B * S, D), eg, eb,
                      w_gate, w_br_diff[0].astype(bf16), w_br_sb[0].astype(bf16), w_out[0].astype(bf16),
                      b_gate[0], ln_g[0].reshape(1, D), ln_b[0].reshape(1, D))
    return out.reshape(B, S, D)
```

```python
import math

import jax
import jax.numpy as jnp
from jax import lax
from jax.experimental import pallas as pl
from jax.experimental.pallas import tpu as pltpu

D_MODEL = 1024
N_META = 16
DIFF_HEADS = 8
DIFF_HEAD_DIM = 64
DIFF_V_DIM = 128
SB_HEADS = 16
SB_HEAD_DIM = 64
DEPTH = 1
DN_ALPHA = (2.0 * DEPTH) ** 0.25
LN_EPS = 1e-5
RMS_EPS = 1e-5
LAM_INIT = 0.8 - 0.6 * math.exp(-0.3 * 0)

LANES = 128
TD = 512
TS = 256
SB_HP = 4
TM_PROJ = 512
TM_FINAL = 256
VMEM_LIMIT = 56 * 1024 * 1024
LOG2E = 1.4426950408889634
NEG_BIG = -1e30
SB_EXIT_LOG = -105.0

_NT = (((1,), (1,)), ((), ()))


def _layer_norm(x, g, b):
    mu = jnp.mean(x, axis=-1, keepdims=True)
    xc = x - mu
    var = jnp.mean(xc * xc, axis=-1, keepdims=True)
    return xc * lax.rsqrt(var + LN_EPS) * g + b


def _proj_kernel(x_ref, g_ref, b_ref, w_ref, wvt_ref, qd_ref, kd_ref, qs_ref, ks_ref, vdt_ref, vst_ref):
    h = _layer_norm(x_ref[...], g_ref[...], b_ref[...]).astype(jnp.bfloat16)
    for c, o_ref in enumerate((qd_ref, kd_ref, qs_ref, ks_ref)):
        o = jnp.dot(h, w_ref[:, c * D_MODEL:(c + 1) * D_MODEL], preferred_element_type=jnp.float32)
        o_ref[...] = o.astype(jnp.bfloat16)
    for c, (o_ref, tk) in enumerate(((vdt_ref, TD), (vst_ref, TS))):
        vt = lax.dot_general(wvt_ref[c * D_MODEL:(c + 1) * D_MODEL, :], h, _NT,
                             preferred_element_type=jnp.float32).astype(jnp.bfloat16)
        for t in range(TM_PROJ // tk):
            o_ref[t] = vt[:, t * tk:(t + 1) * tk]


def _proj_call(x, g, b, w_qk, w_vt):
    B, S, D = x.shape
    nt = S // TM_PROJ
    row_spec = pl.BlockSpec((None, TM_PROJ, D), lambda bi, i: (bi, i, 0))
    vt_specs = [pl.BlockSpec((None, TM_PROJ // tk, D, tk), lambda bi, i: (bi, i, 0, 0)) for tk in (TD, TS)]
    const = lambda bi, i: (0, 0)
    return pl.pallas_call(
        _proj_kernel,
        grid=(B, nt),
        in_specs=[row_spec,
                  pl.BlockSpec((1, D), const), pl.BlockSpec((1, D), const),
                  pl.BlockSpec((D, 4 * D), const), pl.BlockSpec((2 * D, D), const)],
        out_specs=[row_spec, row_spec, row_spec, row_spec] + vt_specs,
        out_shape=[jax.ShapeDtypeStruct((B, S, D), jnp.bfloat16)] * 4
        + [jax.ShapeDtypeStruct((B, S // tk, D, tk), jnp.bfloat16) for tk in (TD, TS)],
        compiler_params=pltpu.CompilerParams(dimension_semantics=("arbitrary", "arbitrary"),
                                             vmem_limit_bytes=VMEM_LIMIT),
        name="proj",
    )(x, g, b, w_qk, w_vt)


def _meta_proj_kernel(m_ref, g_ref, b_ref, w_ref, wvt_ref, k_ref, vt_ref):
    h = _layer_norm(m_ref[...], g_ref[...], b_ref[...]).astype(jnp.bfloat16)
    k_ref[...] = jnp.dot(h, w_ref[...], preferred_element_type=jnp.float32).astype(jnp.bfloat16)
    vt_ref[...] = lax.dot_general(wvt_ref[...], h, _NT,
                                  preferred_element_type=jnp.float32).astype(jnp.bfloat16)


def _meta_proj_call(meta, g, b, w_k, w_vt):
    D = D_MODEL
    return pl.pallas_call(
        _meta_proj_kernel,
        out_shape=[jax.ShapeDtypeStruct((N_META, 2 * D), jnp.bfloat16),
                   jax.ShapeDtypeStruct((2 * D, N_META), jnp.bfloat16)],
        compiler_params=pltpu.CompilerParams(vmem_limit_bytes=VMEM_LIMIT),
        name="meta_proj",
    )(meta, g, b, w_k, w_vt)


def _diff_kernel(slope_ref, lam_ref, q_ref, k_ref, vt_ref, km_ref, vmt_ref, g_ref, o_ref,
                 s_scr, p_scr, acc_scr, m_scr, l_scr, bias_scr):
    S = q_ref.shape[0]
    nq = S // TD
    n_tiles = nq * (nq + 1) // 2
    slope2 = slope_ref[pl.program_id(1)] * LOG2E
    f32, bf16 = jnp.float32, jnp.bfloat16

    row = lax.broadcasted_iota(jnp.int32, (TD, TD), 0)
    col = lax.broadcasted_iota(jnp.int32, (TD, TD), 1)
    bias = slope2 * row.astype(f32)
    bias_scr[0] = bias
    bias_scr[1] = jnp.where(row <= col, bias, NEG_BIG)

    def split(qt):
        zero = jnp.zeros_like(qt)
        ln = lax.broadcasted_iota(jnp.int32, qt.shape, 1)
        return jnp.where(ln < DIFF_HEAD_DIM, qt, zero), jnp.where(ln >= DIFF_HEAD_DIM, qt, zero)

    colq = lax.broadcasted_iota(jnp.int32, (N_META, S), 1)
    rowm = lax.broadcasted_iota(jnp.int32, (N_META, S), 0)
    bias_m = slope2 * (rowm - N_META - (colq - (colq & (TD - 1)))).astype(f32)
    for c, qc in enumerate(split(q_ref[...])):
        s = lax.dot_general(km_ref[...], qc, _NT, preferred_element_type=f32) * LOG2E + bias_m
        m0 = jnp.max(s, axis=0, keepdims=True)
        p = jnp.exp2(s - m0)
        l0 = jnp.sum(p, axis=0, keepdims=True)
        a0 = jnp.dot(vmt_ref[...], p.astype(bf16), preferred_element_type=f32)
        for t in range(nq):
            m_scr[t, c] = m0[:, t * TD:(t + 1) * TD]
            l_scr[t, c] = l0[:, t * TD:(t + 1) * TD]
            acc_scr[t, c] = a0[:, t * TD:(t + 1) * TD]

    def stage_a(qt, kt):
        k_tile = k_ref[pl.ds(pl.multiple_of(kt * TD, TD), TD), :]
        qcs = split(q_ref[pl.ds(pl.multiple_of(qt * TD, TD), TD), :])
        tile_bias = bias_scr[(kt == qt).astype(jnp.int32)]
        mx = []
        for c in range(2):
            s = lax.dot_general(k_tile, qcs[c], _NT, preferred_element_type=f32) * LOG2E + tile_bias
            s_scr[c] = s
            mx.append(jnp.max(s, axis=0, keepdims=True))
        return tuple(mx)

    def stage_b(qt, kt, mx):
        off2 = -(slope2 * TD) * (qt - kt).astype(f32)
        alphas = []
        for c in range(2):
            m_old = m_scr[qt, c]
            m_new = jnp.maximum(m_old, mx[c] + off2)
            alpha = jnp.exp2(m_old - m_new)
            p = jnp.exp2(s_scr[c] - (m_new - off2))
            l_scr[qt, c] = alpha * l_scr[qt, c] + jnp.sum(p, axis=0, keepdims=True)
            m_scr[qt, c] = m_new
            p_scr[c] = p.astype(bf16)
            alphas.append(alpha)
        return tuple(alphas)

    def stage_c(qt, kt, alphas):
        vt_tile = vt_ref[kt]
        for c in range(2):
            acc_scr[qt, c] = alphas[c] * acc_scr[qt, c] + jnp.dot(vt_tile, p_scr[c], preferred_element_type=f32)

    lam = (jnp.exp(jnp.sum(lam_ref[0:1, :] * lam_ref[1:2, :], axis=1, keepdims=True))
           - jnp.exp(jnp.sum(lam_ref[2:3, :] * lam_ref[3:4, :], axis=1, keepdims=True)) + LAM_INIT)

    def finalize(qt):
        y = (acc_scr[qt, 0] * (1.0 / l_scr[qt, 0]) - lam * (acc_scr[qt, 1] * (1.0 / l_scr[qt, 1])))
        ms = jnp.mean(y * y, axis=0, keepdims=True)
        y = y * lax.rsqrt(ms + RMS_EPS) * g_ref[...] * (1.0 - LAM_INIT)
        o_ref[pl.ds(pl.multiple_of(qt * TD, TD), TD), :] = y.T.astype(o_ref.dtype)

    def next_tile(qt, kt):
        last = kt == 0
        return jnp.where(last, qt + 1, qt), jnp.where(last, qt + 1, kt - 1)

    p_scr[...] = jnp.zeros(p_scr.shape, bf16)
    ones = jnp.ones((1, TD), f32)
    zero_i = jnp.int32(0)
    mx0 = stage_a(zero_i, zero_i)

    def body(u, st):
        qa, ka, qb, kb, qc, kc, mx, alphas = st
        stage_c(qc, kc, alphas)
        alphas = stage_b(qb, kb, mx)
        mx = stage_a(jnp.minimum(qa, nq - 1), jnp.minimum(ka, nq - 1))

        @pl.when(jnp.logical_and(u > 0, kc == 0))
        def _():
            finalize(qc)

        qn, kn = next_tile(qa, ka)
        return qn, kn, qa, ka, qb, kb, mx, alphas

    st = (jnp.int32(1), jnp.int32(1), zero_i, zero_i, zero_i, zero_i, mx0, (ones, ones))
    st = lax.fori_loop(0, n_tiles, body, st)
    _, _, _, _, qc, kc, _, alphas = st
    stage_c(qc, kc, alphas)
    finalize(qc)


def _diff_call(slopes, lam, qd, kd, vdt, kdm, vdmt, g_col):
    B, S, D = qd.shape
    nq = S // TD
    smem = pl.BlockSpec(memory_space=pltpu.SMEM)
    slab = pl.BlockSpec((None, S, LANES), lambda b, h: (b, 0, h))
    return pl.pallas_call(
        _diff_kernel,
        grid=(B, DIFF_HEADS),
        in_specs=[smem, pl.BlockSpec((4, DIFF_HEAD_DIM), lambda b, h: (0, 0)),
                  slab, slab,
                  pl.BlockSpec((None, nq, DIFF_V_DIM, TD), lambda b, h: (b, 0, h, 0)),
                  pl.BlockSpec((N_META, LANES), lambda b, h: (0, h)),
                  pl.BlockSpec((DIFF_V_DIM, N_META), lambda b, h: (h, 0)),
                  pl.BlockSpec((DIFF_V_DIM, 1), lambda b, h: (0, 0))],
        out_specs=slab,
        out_shape=jax.ShapeDtypeStruct((B, S, D), jnp.bfloat16),
        scratch_shapes=[pltpu.VMEM((2, TD, TD), jnp.float32),
                        pltpu.VMEM((2, TD, TD), jnp.bfloat16),
                        pltpu.VMEM((nq, 2, DIFF_V_DIM, TD), jnp.float32),
                        pltpu.VMEM((nq, 2, 1, TD), jnp.float32),
                        pltpu.VMEM((nq, 2, 1, TD), jnp.float32),
                        pltpu.VMEM((2, TD, TD), jnp.float32)],
        compiler_params=pltpu.CompilerParams(dimension_semantics=("arbitrary",) * 2,
                                             vmem_limit_bytes=VMEM_LIMIT),
        name="diff_attn",
    )(slopes, lam, qd, kd, vdt, kdm, vdmt, g_col)


def _sb_kernel(q_ref, k_ref, vt_ref, km_ref, vmt_ref, u_ref, um_ref, o_ref):
    qi = pl.program_id(2)
    q = q_ref[...]
    lane = lax.broadcasted_iota(jnp.int32, q.shape, 1)
    zero = jnp.zeros_like(q)
    row = lax.broadcasted_iota(jnp.int32, (TS, TS), 0)
    col = lax.broadcasted_iota(jnp.int32, (TS, TS), 1)
    strict = row < col

    def logits(k_tile, qh):
        return lax.dot_general(k_tile, qh, _NT, preferred_element_type=jnp.float32)

    def split_softplus(z, mask):
        e = jnp.exp2(jnp.abs(z) * (-LOG2E))
        sp = jnp.maximum(z, 0.0) + jnp.log(1.0 + e)
        if mask is not None:
            sp = jnp.where(mask, sp, 0.0)
        hi = sp.astype(jnp.bfloat16)
        return hi, (sp - hi.astype(jnp.float32)).astype(jnp.bfloat16)

    def suffix(u, hi, lo):
        return jnp.dot(u, hi, preferred_element_type=jnp.float32) + jnp.dot(u, lo, preferred_element_type=jnp.float32)

    def weights(z, incl, carry, mask):
        a = jnp.exp(z + incl + carry)
        if mask is not None:
            a = jnp.where(mask, a, 0.0)
        return a.astype(jnp.bfloat16)

    def tile(k_tile, v_tile, u, qh, carry, acc):
        z = logits(k_tile, qh)
        incl = suffix(u, *split_softplus(z, None))
        pv = jnp.dot(v_tile, weights(z, incl, carry, None), preferred_element_type=jnp.float32)
        return carry + incl[0:1, :], acc + pv

    has_left = qi > 0
    jl = jnp.maximum(qi - 1, 0)
    k0 = pl.multiple_of(qi * TS, TS)
    kl = pl.multiple_of(jl * TS, TS)
    u = u_ref[...]
    qhs = [jnp.where(jnp.logical_and(lane >= hh * SB_HEAD_DIM, lane < (hh + 1) * SB_HEAD_DIM), q, zero)
           for hh in range(SB_HP)]

    chains = [(hh, strict, k0, qi) for hh in range(SB_HP)] + [(hh, None, kl, jl) for hh in range(SB_HP)]
    zs = [logits(k_ref[pl.ds(kk, TS), :], qhs[hh]) for hh, _, kk, _ in chains]
    parts = [split_softplus(z, mask) for z, (_, mask, _, _) in zip(zs, chains)]
    incls = [suffix(u, hi, lo) for hi, lo in parts]
    tots = [incl[0:1, :] for incl in incls]
    carries = [0.0] * SB_HP + [jnp.where(has_left, tots[hh], NEG_BIG) for hh in range(SB_HP)]
    ws = [weights(z, incl, c, mask) for z, incl, c, (_, mask, _, _) in zip(zs, incls, carries, chains)]
    pvs = [jnp.dot(vt_ref[j, hh * SB_HEAD_DIM:(hh + 1) * SB_HEAD_DIM, :], w, preferred_element_type=jnp.float32)
           for w, (hh, _, _, j) in zip(ws, chains)]
    near = [(qhs[hh], hh * SB_HEAD_DIM, tots[hh] + jnp.where(has_left, tots[SB_HP + hh], 0.0),
             pvs[hh] + pvs[SB_HP + hh]) for hh in range(SB_HP)]

    outs = []
    for qh, r0, carry, acc in near:

        def cond(st):
            j, cmax, _, _ = st
            return jnp.logical_and(j >= 0, cmax > SB_EXIT_LOG)

        def body(st):
            j, _, carry, acc = st
            kk = pl.multiple_of(j * TS, TS)
            carry, acc = tile(k_ref[pl.ds(kk, TS), :], vt_ref[j, r0:r0 + SB_HEAD_DIM, :], u, qh, carry, acc)
            return j - 1, jnp.max(carry), carry, acc

        _, cmax, carry, acc = lax.while_loop(cond, body, (qi - 2, jnp.max(carry), carry, acc))

        def meta(acc):
            return tile(km_ref[...], vmt_ref[r0:r0 + SB_HEAD_DIM, :], um_ref[...], qh, carry, acc)[1]

        acc = lax.cond(cmax > SB_EXIT_LOG, meta, lambda a: a, acc)
        outs.append(acc)

    o_ref[...] = jnp.concatenate(outs, axis=0).T.astype(o_ref.dtype)


def _sb_call(qs, ks, vst, ksm, vsmt, u, um):
    B, S, D = qs.shape
    nq = S // TS
    width = SB_HP * SB_HEAD_DIM
    const = lambda b, h, i: (0, 0)
    return pl.pallas_call(
        _sb_kernel,
        grid=(B, SB_HEADS // SB_HP, nq),
        in_specs=[pl.BlockSpec((None, TS, width), lambda b, h, i: (b, i, h)),
                  pl.BlockSpec((None, S, width), lambda b, h, i: (b, 0, h)),
                  pl.BlockSpec((None, S // TS, width, TS), lambda b, h, i: (b, 0, h, 0)),
                  pl.BlockSpec((N_META, width), lambda b, h, i: (0, h)),
                  pl.BlockSpec((width, N_META), lambda b, h, i: (h, 0)),
                  pl.BlockSpec((TS, TS), const), pl.BlockSpec((N_META, N_META), const)],
        out_specs=pl.BlockSpec((None, TS, width), lambda b, h, i: (b, i, h)),
        out_shape=jax.ShapeDtypeStruct((B, S, D), jnp.bfloat16),
        compiler_params=pltpu.CompilerParams(dimension_semantics=("arbitrary",) * 3,
                                             vmem_limit_bytes=VMEM_LIMIT),
        name="sb_attn",
    )(qs, ks, vst, ksm, vsmt, u, um)


def _sigmoid(x):
    return 1.0 / (1.0 + jnp.exp(-x))


def _final_kernel(x_ref, yd_ref, ys_ref, eg_ref, eb_ref, wg_ref, wbd_ref, wbs_ref, wo_ref, bg_ref,
                  lg_ref, lb_ref, o_ref):
    D = D_MODEL
    h = _layer_norm(x_ref[...], eg_ref[...], eb_ref[...])
    hb = h.astype(jnp.bfloat16)

    def gate(c):
        return jnp.dot(hb, wg_ref[:, c * D:(c + 1) * D], preferred_element_type=jnp.float32)

    g_d = gate(0)
    u_d = (yd_ref[...].astype(jnp.float32) * (g_d * _sigmoid(g_d))).astype(jnp.bfloat16)
    br_d = jnp.dot(u_d, wbd_ref[...], preferred_element_type=jnp.float32)
    merged = _sigmoid(gate(2) + bg_ref[0:1, :]) * br_d
    g_s = gate(1)
    u_s = (ys_ref[...].astype(jnp.float32) * (g_s * _sigmoid(g_s))).astype(jnp.bfloat16)
    br_s = jnp.dot(u_s, wbs_ref[...], preferred_element_type=jnp.float32)
    merged = merged + _sigmoid(gate(3) + bg_ref[1:2, :]) * br_s
    y = jnp.dot(merged.astype(jnp.bfloat16), wo_ref[...], preferred_element_type=jnp.float32)
    o_ref[...] = _layer_norm(DN_ALPHA * h + y, lg_ref[...], lb_ref[...]).astype(o_ref.dtype)


def _final_call(x2, yd2, ys2, eg, eb, wg, wbd, wbs, wo, bg, lg, lb):
    R, D = x2.shape
    row = pl.BlockSpec((TM_FINAL, D), lambda i: (i, 0))
    const = lambda i: (0, 0)
    vec = pl.BlockSpec((1, D), const)
    return pl.pallas_call(
        _final_kernel,
        grid=(R // TM_FINAL,),
        in_specs=[row, row, row, vec, vec,
                  pl.BlockSpec((D, 4 * D), const), pl.BlockSpec((D, D), const),
                  pl.BlockSpec((D, D), const), pl.BlockSpec((D, D), const),
                  pl.BlockSpec((2, D), const), vec, vec],
        out_specs=row,
        out_shape=jax.ShapeDtypeStruct((R, D), jnp.float32),
        compiler_params=pltpu.CompilerParams(dimension_semantics=("arbitrary",),
                                             vmem_limit_bytes=VMEM_LIMIT),
        name="final",
    )(x2, yd2, ys2, eg, eb, wg, wbd, wbs, wo, bg, lg, lb)


def _neg_upper_ones(n):
    r = jnp.arange(n)
    return jnp.where(r[None, :] >= r[:, None], -1.0, 0.0).astype(jnp.bfloat16)


def kernel(x, meta_tokens, emb_ln_g, emb_ln_b, w_in, b_gate, diff_lambda, diff_subln_g,
           w_br_diff, w_br_sb, w_out, ln_g, ln_b):
    B, S, D = x.shape
    bf16 = jnp.bfloat16
    w = w_in[0]
    col = lambda i: w[:, i * D:(i + 1) * D]
    qscale = DIFF_HEAD_DIM ** -0.5
    w_qk = jnp.concatenate([col(0) * qscale, col(1), col(4) * qscale, col(5)], axis=1).astype(bf16)
    w_vt = jnp.concatenate([col(2).T, col(6).T], axis=0).astype(bf16)
    w_kmeta = jnp.concatenate([col(1), col(5)], axis=1).astype(bf16)
    w_gate = jnp.concatenate([col(3), col(7), col(8), col(9)], axis=1).astype(bf16)
    eg, eb = emb_ln_g.reshape(1, D), emb_ln_b.reshape(1, D)

    qd, kd, qs, ks, vdt, vst = _proj_call(x, eg, eb, w_qk, w_vt)
    k_meta, vt_meta = _meta_proj_call(meta_tokens, eg, eb, w_kmeta, w_vt)

    slopes = 2.0 ** (-8.0 * jnp.arange(1, DIFF_HEADS + 1, dtype=jnp.float32) / DIFF_HEADS)

    yd = _diff_call(slopes, diff_lambda[0], qd, kd, vdt, k_meta[:, :D], vt_meta[:D], diff_subln_g[0].reshape(DIFF_V_DIM, 1))
    ys = _sb_call(qs, ks, vst, k_meta[:, D:], vt_meta[D:], _neg_upper_ones(TS), _neg_upper_ones(N_META))

    out = _final_call(x.reshape(B * S, D), yd.reshape(B * S, D), ys.reshape(B * S, D), eg, eb,
                      w_gate, w_br_diff[0].astype(bf16), w_br_sb[0].astype(bf16), w_out[0].astype(bf16),
                      b_gate[0], ln_g[0].reshape(1, D), ln_b[0].reshape(1, D))
    return out.reshape(B, S, D)
```

```python
import functools
import math

import jax
import jax.numpy as jnp
from jax import lax
from jax.experimental import pallas as pl
from jax.experimental.pallas import tpu as pltpu

D_MODEL = 1024
N_META = 16
DIFF_HEADS = 8
DIFF_HEAD_DIM = 64
DIFF_V_DIM = 128
SB_HEADS = 16
SB_HEAD_DIM = 64
DEPTH = 1
DN_ALPHA = (2.0 * DEPTH) ** 0.25
LN_EPS = 1e-5
RMS_EPS = 1e-5
LAM_INIT = 0.8 - 0.6 * math.exp(-0.3 * 0)

LANES = 128
TD = 512
TS = 256
SB_HP = 4
TM_PROJ = 512
TM_FINAL = 256
VMEM_LIMIT = 56 * 1024 * 1024
LOG2E = 1.4426950408889634
NEG_BIG = -1e30
SB_EXIT_LOG = -105.0

_NT = (((1,), (1,)), ((), ()))


def _layer_norm(x, g, b):
    mu = jnp.mean(x, axis=-1, keepdims=True)
    xc = x - mu
    var = jnp.mean(xc * xc, axis=-1, keepdims=True)
    return xc * lax.rsqrt(var + LN_EPS) * g + b


def _proj_kernel(x_ref, g_ref, b_ref, w_ref, wvt_ref, qd_ref, kd_ref, qs_ref, ks_ref, vdt_ref, vst_ref):
    h = _layer_norm(x_ref[...], g_ref[...], b_ref[...]).astype(jnp.bfloat16)
    for c, o_ref in enumerate((qd_ref, kd_ref, qs_ref, ks_ref)):
        o = jnp.dot(h, w_ref[:, c * D_MODEL:(c + 1) * D_MODEL], preferred_element_type=jnp.float32)
        o_ref[...] = o.astype(jnp.bfloat16)
    for c, (o_ref, tk) in enumerate(((vdt_ref, TD), (vst_ref, TS))):
        vt = lax.dot_general(wvt_ref[c * D_MODEL:(c + 1) * D_MODEL, :], h, _NT,
                             preferred_element_type=jnp.float32).astype(jnp.bfloat16)
        for t in range(TM_PROJ // tk):
            o_ref[t] = vt[:, t * tk:(t + 1) * tk]


def _proj_call(x, g, b, w_qk, w_vt):
    B, S, D = x.shape
    nt = S // TM_PROJ
    row_spec = pl.BlockSpec((None, TM_PROJ, D), lambda bi, i: (bi, i, 0))
    vt_specs = [pl.BlockSpec((None, TM_PROJ // tk, D, tk), lambda bi, i: (bi, i, 0, 0)) for tk in (TD, TS)]
    const = lambda bi, i: (0, 0)
    return pl.pallas_call(
        _proj_kernel,
        grid=(B, nt),
        in_specs=[row_spec,
                  pl.BlockSpec((1, D), const), pl.BlockSpec((1, D), const),
                  pl.BlockSpec((D, 4 * D), const), pl.BlockSpec((2 * D, D), const)],
        out_specs=[row_spec, row_spec, row_spec, row_spec] + vt_specs,
        out_shape=[jax.ShapeDtypeStruct((B, S, D), jnp.bfloat16)] * 4
        + [jax.ShapeDtypeStruct((B, S // tk, D, tk), jnp.bfloat16) for tk in (TD, TS)],
        compiler_params=pltpu.CompilerParams(dimension_semantics=("arbitrary", "arbitrary"),
                                             vmem_limit_bytes=VMEM_LIMIT),
        name="proj",
    )(x, g, b, w_qk, w_vt)


def _meta_proj_kernel(m_ref, g_ref, b_ref, w_ref, wvt_ref, k_ref, vt_ref):
    h = _layer_norm(m_ref[...], g_ref[...], b_ref[...]).astype(jnp.bfloat16)
    k_ref[...] = jnp.dot(h, w_ref[...], preferred_element_type=jnp.float32).astype(jnp.bfloat16)
    vt_ref[...] = lax.dot_general(wvt_ref[...], h, _NT,
                                  preferred_element_type=jnp.float32).astype(jnp.bfloat16)


def _meta_proj_call(meta, g, b, w_k, w_vt):
    D = D_MODEL
    return pl.pallas_call(
        _meta_proj_kernel,
        out_shape=[jax.ShapeDtypeStruct((N_META, 2 * D), jnp.bfloat16),
                   jax.ShapeDtypeStruct((2 * D, N_META), jnp.bfloat16)],
        compiler_params=pltpu.CompilerParams(vmem_limit_bytes=VMEM_LIMIT),
        name="meta_proj",
    )(meta, g, b, w_k, w_vt)


def _diff_kernel(slope_ref, lam_ref, q_ref, k_ref, vt_ref, km_ref, vmt_ref, g_ref, o_ref,
                 s_scr, p_scr, acc_scr, m_scr, l_scr, bias_scr):
    S = q_ref.shape[0]
    nq = S // TD
    n_tiles = nq * (nq + 1) // 2
    slope2 = slope_ref[pl.program_id(1)] * LOG2E
    f32, bf16 = jnp.float32, jnp.bfloat16

    row = lax.broadcasted_iota(jnp.int32, (TD, TD), 0)
    col = lax.broadcasted_iota(jnp.int32, (TD, TD), 1)
    bias = slope2 * row.astype(f32)
    bias_scr[0] = bias
    bias_scr[1] = jnp.where(row <= col, bias, NEG_BIG)

    def split(qt):
        zero = jnp.zeros_like(qt)
        ln = lax.broadcasted_iota(jnp.int32, qt.shape, 1)
        return jnp.where(ln < DIFF_HEAD_DIM, qt, zero), jnp.where(ln >= DIFF_HEAD_DIM, qt, zero)

    colq = lax.broadcasted_iota(jnp.int32, (N_META, S), 1)
    rowm = lax.broadcasted_iota(jnp.int32, (N_META, S), 0)
    bias_m = slope2 * (rowm - N_META - (colq - (colq & (TD - 1)))).astype(f32)
    for c, qc in enumerate(split(q_ref[...])):
        s = lax.dot_general(km_ref[...], qc, _NT, preferred_element_type=f32) + bias_m
        m0 = jnp.max(s, axis=0, keepdims=True)
        p = jnp.exp2(s - m0)
        l0 = jnp.sum(p, axis=0, keepdims=True)
        a0 = jnp.dot(vmt_ref[...], p.astype(bf16), preferred_element_type=f32)
        for t in range(nq):
            m_scr[t, c] = m0[:, t * TD:(t + 1) * TD]
            l_scr[t, c] = l0[:, t * TD:(t + 1) * TD]
            acc_scr[t, c] = a0[:, t * TD:(t + 1) * TD]

    def stage_a(qt, kt):
        k_tile = k_ref[pl.ds(pl.multiple_of(kt * TD, TD), TD), :]
        qcs = split(q_ref[pl.ds(pl.multiple_of(qt * TD, TD), TD), :])
        tile_bias = bias_scr[(kt == qt).astype(jnp.int32)]
        mx = []
        for c in range(2):
            s = lax.dot_general(k_tile, qcs[c], _NT, preferred_element_type=f32) + tile_bias
            s_scr[c] = s
            mx.append(jnp.max(s, axis=0, keepdims=True))
        return tuple(mx)

    def stage_b(qt, kt, mx):
        off2 = -(slope2 * TD) * (qt - kt).astype(f32)
        alphas = []
        for c in range(2):
            m_old = m_scr[qt, c]
            m_new = jnp.maximum(m_old, mx[c] + off2)
            alpha = jnp.exp2(m_old - m_new)
            p = jnp.exp2(s_scr[c] - (m_new - off2))
            l_scr[qt, c] = alpha * l_scr[qt, c] + jnp.sum(p, axis=0, keepdims=True)
            m_scr[qt, c] = m_new
            p_scr[c] = p.astype(bf16)
            alphas.append(alpha)
        return tuple(alphas)

    def stage_c(qt, kt, alphas):
        vt_tile = vt_ref[kt]
        for c in range(2):
            acc_scr[qt, c] = alphas[c] * acc_scr[qt, c] + jnp.dot(vt_tile, p_scr[c], preferred_element_type=f32)

    lam = (jnp.exp(jnp.sum(lam_ref[0:1, :] * lam_ref[1:2, :], axis=1, keepdims=True))
           - jnp.exp(jnp.sum(lam_ref[2:3, :] * lam_ref[3:4, :], axis=1, keepdims=True)) + LAM_INIT)

    def finalize(qt):
        y = (acc_scr[qt, 0] * (1.0 / l_scr[qt, 0]) - lam * (acc_scr[qt, 1] * (1.0 / l_scr[qt, 1])))
        ms = jnp.mean(y * y, axis=0, keepdims=True)
        y = y * lax.rsqrt(ms + RMS_EPS) * g_ref[...] * (1.0 - LAM_INIT)
        o_ref[pl.ds(pl.multiple_of(qt * TD, TD), TD), :] = y.T.astype(o_ref.dtype)

    def next_tile(qt, kt):
        last = kt == 0
        return jnp.where(last, qt + 1, qt), jnp.where(last, qt + 1, kt - 1)

    p_scr[...] = jnp.zeros(p_scr.shape, bf16)
    ones = jnp.ones((1, TD), f32)
    zero_i = jnp.int32(0)
    mx0 = stage_a(zero_i, zero_i)

    def body(u, st):
        qa, ka, qb, kb, qc, kc, mx, alphas = st
        stage_c(qc, kc, alphas)
        alphas = stage_b(qb, kb, mx)
        mx = stage_a(jnp.minimum(qa, nq - 1), jnp.minimum(ka, nq - 1))

        @pl.when(jnp.logical_and(u > 0, kc == 0))
        def _():
            finalize(qc)

        qn, kn = next_tile(qa, ka)
        return qn, kn, qa, ka, qb, kb, mx, alphas

    st = (jnp.int32(1), jnp.int32(1), zero_i, zero_i, zero_i, zero_i, mx0, (ones, ones))
    st = lax.fori_loop(0, n_tiles, body, st)
    _, _, _, _, qc, kc, _, alphas = st
    stage_c(qc, kc, alphas)
    finalize(qc)


def _diff_call(slopes, lam, qd, kd, vdt, kdm, vdmt, g_col):
    B, S, D = qd.shape
    nq = S // TD
    smem = pl.BlockSpec(memory_space=pltpu.SMEM)
    slab = pl.BlockSpec((None, S, LANES), lambda b, h: (b, 0, h))
    return pl.pallas_call(
        _diff_kernel,
        grid=(B, DIFF_HEADS),
        in_specs=[smem, pl.BlockSpec((4, DIFF_HEAD_DIM), lambda b, h: (0, 0)),
                  slab, slab,
                  pl.BlockSpec((None, nq, DIFF_V_DIM, TD), lambda b, h: (b, 0, h, 0)),
                  pl.BlockSpec((N_META, LANES), lambda b, h: (0, h)),
                  pl.BlockSpec((DIFF_V_DIM, N_META), lambda b, h: (h, 0)),
                  pl.BlockSpec((DIFF_V_DIM, 1), lambda b, h: (0, 0))],
        out_specs=slab,
        out_shape=jax.ShapeDtypeStruct((B, S, D), jnp.bfloat16),
        scratch_shapes=[pltpu.VMEM((2, TD, TD), jnp.float32),
                        pltpu.VMEM((2, TD, TD), jnp.bfloat16),
                        pltpu.VMEM((nq, 2, DIFF_V_DIM, TD), jnp.float32),
                        pltpu.VMEM((nq, 2, 1, TD), jnp.float32),
                        pltpu.VMEM((nq, 2, 1, TD), jnp.float32),
                        pltpu.VMEM((2, TD, TD), jnp.float32)],
        compiler_params=pltpu.CompilerParams(dimension_semantics=("arbitrary",) * 2,
                                             vmem_limit_bytes=VMEM_LIMIT),
        name="diff_attn",
    )(slopes, lam, qd, kd, vdt, kdm, vdmt, g_col)


def _sb_kernel(q_ref, k_ref, vt_ref, km_ref, vmt_ref, u_ref, um_ref, o_ref):
    qi = pl.program_id(2)
    q = q_ref[...]
    lane = lax.broadcasted_iota(jnp.int32, q.shape, 1)
    zero = jnp.zeros_like(q)
    row = lax.broadcasted_iota(jnp.int32, (TS, TS), 0)
    col = lax.broadcasted_iota(jnp.int32, (TS, TS), 1)
    strict = row < col

    def logits(k_tile, qh):
        return lax.dot_general(k_tile, qh, _NT, preferred_element_type=jnp.float32)

    def split_softplus(z, mask):
        e = jnp.exp2(jnp.abs(z) * (-LOG2E))
        sp = jnp.maximum(z, 0.0) + jnp.log(1.0 + e)
        if mask is not None:
            sp = jnp.where(mask, sp, 0.0)
        hi = sp.astype(jnp.bfloat16)
        return hi, (sp - hi.astype(jnp.float32)).astype(jnp.bfloat16)

    def suffix(u, hi, lo):
        return jnp.dot(u, hi, preferred_element_type=jnp.float32) + jnp.dot(u, lo, preferred_element_type=jnp.float32)

    def weights(z, incl, carry, mask):
        a = jnp.exp(z + incl + carry)
        if mask is not None:
            a = jnp.where(mask, a, 0.0)
        return a.astype(jnp.bfloat16)

    def tile(k_tile, v_tile, u, qh, carry, acc):
        z = logits(k_tile, qh)
        incl = suffix(u, *split_softplus(z, None))
        pv = jnp.dot(v_tile, weights(z, incl, carry, None), preferred_element_type=jnp.float32)
        return carry + incl[0:1, :], acc + pv

    has_left = qi > 0
    jl = jnp.maximum(qi - 1, 0)
    k0 = pl.multiple_of(qi * TS, TS)
    kl = pl.multiple_of(jl * TS, TS)
    u = u_ref[...]
    qhs = [jnp.where(jnp.logical_and(lane >= hh * SB_HEAD_DIM, lane < (hh + 1) * SB_HEAD_DIM), q, zero)
           for hh in range(SB_HP)]

    chains = [(hh, strict, k0, qi) for hh in range(SB_HP)] + [(hh, None, kl, jl) for hh in range(SB_HP)]
    zs = [logits(k_ref[pl.ds(kk, TS), :], qhs[hh]) for hh, _, kk, _ in chains]
    parts = [split_softplus(z, mask) for z, (_, mask, _, _) in zip(zs, chains)]
    incls = [suffix(u, hi, lo) for hi, lo in parts]
    tots = [incl[0:1, :] for incl in incls]
    carries = [0.0] * SB_HP + [jnp.where(has_left, tots[hh], NEG_BIG) for hh in range(SB_HP)]
    ws = [weights(z, incl, c, mask) for z, incl, c, (_, mask, _, _) in zip(zs, incls, carries, chains)]
    pvs = [jnp.dot(vt_ref[j, hh * SB_HEAD_DIM:(hh + 1) * SB_HEAD_DIM, :], w, preferred_element_type=jnp.float32)
           for w, (hh, _, _, j) in zip(ws, chains)]
    near = [(qhs[hh], hh * SB_HEAD_DIM, tots[hh] + jnp.where(has_left, tots[SB_HP + hh], 0.0),
             pvs[hh] + pvs[SB_HP + hh]) for hh in range(SB_HP)]

    def further_tiles(accs):
        outs = []
        for (qh, r0, carry, _), acc in zip(near, accs):

            def cond(st):
                j, cmax, _, _ = st
                return jnp.logical_and(j >= 0, cmax > SB_EXIT_LOG)

            def body(st):
                j, _, carry, acc = st
                kk = pl.multiple_of(j * TS, TS)
                carry, acc = tile(k_ref[pl.ds(kk, TS), :], vt_ref[j, r0:r0 + SB_HEAD_DIM, :], u, qh, carry, acc)
                return j - 1, jnp.max(carry), carry, acc

            _, cmax, carry, acc = lax.while_loop(cond, body, (qi - 2, jnp.max(carry), carry, acc))

            def meta(acc):
                return tile(km_ref[...], vmt_ref[r0:r0 + SB_HEAD_DIM, :], um_ref[...], qh, carry, acc)[1]

            outs.append(lax.cond(cmax > SB_EXIT_LOG, meta, lambda a: a, acc))
        return tuple(outs)

    worst = functools.reduce(jnp.maximum, [carry for _, _, carry, _ in near])
    accs = lax.cond(jnp.max(worst) > SB_EXIT_LOG, further_tiles, lambda a: a, tuple(acc for _, _, _, acc in near))

    o_ref[...] = jnp.concatenate(accs, axis=0).T.astype(o_ref.dtype)


def _sb_call(qs, ks, vst, ksm, vsmt, u, um):
    B, S, D = qs.shape
    nq = S // TS
    width = SB_HP * SB_HEAD_DIM
    const = lambda b, h, i: (0, 0)
    return pl.pallas_call(
        _sb_kernel,
        grid=(B, SB_HEADS // SB_HP, nq),
        in_specs=[pl.BlockSpec((None, TS, width), lambda b, h, i: (b, i, h)),
                  pl.BlockSpec((None, S, width), lambda b, h, i: (b, 0, h)),
                  pl.BlockSpec((None, S // TS, width, TS), lambda b, h, i: (b, 0, h, 0)),
                  pl.BlockSpec((N_META, width), lambda b, h, i: (0, h)),
                  pl.BlockSpec((width, N_META), lambda b, h, i: (h, 0)),
                  pl.BlockSpec((TS, TS), const), pl.BlockSpec((N_META, N_META), const)],
        out_specs=pl.BlockSpec((None, TS, width), lambda b, h, i: (b, i, h)),
        out_shape=jax.ShapeDtypeStruct((B, S, D), jnp.bfloat16),
        compiler_params=pltpu.CompilerParams(dimension_semantics=("arbitrary",) * 3,
                                             vmem_limit_bytes=VMEM_LIMIT),
        name="sb_attn",
    )(qs, ks, vst, ksm, vsmt, u, um)


def _sigmoid(x):
    return 1.0 / (1.0 + jnp.exp(-x))


def _final_kernel(x_ref, yd_ref, ys_ref, eg_ref, eb_ref, wg_ref, wbd_ref, wbs_ref, wo_ref, bg_ref,
                  lg_ref, lb_ref, o_ref):
    D = D_MODEL
    h = _layer_norm(x_ref[...], eg_ref[...], eb_ref[...])
    hb = h.astype(jnp.bfloat16)

    def gate(c):
        return jnp.dot(hb, wg_ref[:, c * D:(c + 1) * D], preferred_element_type=jnp.float32)

    g_d = gate(0)
    u_d = (yd_ref[...].astype(jnp.float32) * (g_d * _sigmoid(g_d))).astype(jnp.bfloat16)
    br_d = jnp.dot(u_d, wbd_ref[...], preferred_element_type=jnp.float32)
    merged = _sigmoid(gate(2) + bg_ref[0:1, :]) * br_d
    g_s = gate(1)
    u_s = (ys_ref[...].astype(jnp.float32) * (g_s * _sigmoid(g_s))).astype(jnp.bfloat16)
    br_s = jnp.dot(u_s, wbs_ref[...], preferred_element_type=jnp.float32)
    merged = merged + _sigmoid(gate(3) + bg_ref[1:2, :]) * br_s
    y = jnp.dot(merged.astype(jnp.bfloat16), wo_ref[...], preferred_element_type=jnp.float32)
    o_ref[...] = _layer_norm(DN_ALPHA * h + y, lg_ref[...], lb_ref[...]).astype(o_ref.dtype)


def _final_call(x2, yd2, ys2, eg, eb, wg, wbd, wbs, wo, bg, lg, lb):
    R, D = x2.shape
    row = pl.BlockSpec((TM_FINAL, D), lambda i: (i, 0))
    const = lambda i: (0, 0)
    vec = pl.BlockSpec((1, D), const)
    return pl.pallas_call(
        _final_kernel,
        grid=(R // TM_FINAL,),
        in_specs=[row, row, row, vec, vec,
                  pl.BlockSpec((D, 4 * D), const), pl.BlockSpec((D, D), const),
                  pl.BlockSpec((D, D), const), pl.BlockSpec((D, D), const),
                  pl.BlockSpec((2, D), const), vec, vec],
        out_specs=row,
        out_shape=jax.ShapeDtypeStruct((R, D), jnp.float32),
        compiler_params=pltpu.CompilerParams(dimension_semantics=("arbitrary",),
                                             vmem_limit_bytes=VMEM_LIMIT),
        name="final",
    )(x2, yd2, ys2, eg, eb, wg, wbd, wbs, wo, bg, lg, lb)


def _neg_upper_ones(n):
    r = jnp.arange(n)
    return jnp.where(r[None, :] >= r[:, None], -1.0, 0.0).astype(jnp.bfloat16)


def kernel(x, meta_tokens, emb_ln_g, emb_ln_b, w_in, b_gate, diff_lambda, diff_subln_g,
           w_br_diff, w_br_sb, w_out, ln_g, ln_b):
    B, S, D = x.shape
    bf16 = jnp.bfloat16
    w = w_in[0]
    col = lambda i: w[:, i * D:(i + 1) * D]
    qscale = DIFF_HEAD_DIM ** -0.5
    w_qk = jnp.concatenate([col(0) * (qscale * LOG2E), col(1), col(4) * qscale, col(5)], axis=1).astype(bf16)
    w_vt = jnp.concatenate([col(2).T, col(6).T], axis=0).astype(bf16)
    w_kmeta = jnp.concatenate([col(1), col(5)], axis=1).astype(bf16)
    w_gate = jnp.concatenate([col(3), col(7), col(8), col(9)], axis=1).astype(bf16)
    eg, eb = emb_ln_g.reshape(1, D), emb_ln_b.reshape(1, D)

    qd, kd, qs, ks, vdt, vst = _proj_call(x, eg, eb, w_qk, w_vt)
    k_meta, vt_meta = _meta_proj_call(meta_tokens, eg, eb, w_kmeta, w_vt)

    slopes = 2.0 ** (-8.0 * jnp.arange(1, DIFF_HEADS + 1, dtype=jnp.float32) / DIFF_HEADS)

    yd = _diff_call(slopes, diff_lambda[0], qd, kd, vdt, k_meta[:, :D], vt_meta[:D], diff_subln_g[0].reshape(DIFF_V_DIM, 1))
    ys = _sb_call(qs, ks, vst, k_meta[:, D:], vt_meta[D:], _neg_upper_ones(TS), _neg_upper_ones(N_META))

    out = _final_call(x.reshape(B * S, D), yd.reshape(B * S, D), ys.reshape(B * S, D), eg, eb,
                      w_gate, w_br_diff[0].astype(bf16), w_br_sb[0].astype(bf16), w_out[0].astype(bf16),
                      b_gate[0], ln_g[0].reshape(1, D), ln_b[0].reshape(1, D))
    return out.reshape(B, S, D)
```

```python
import functools
import math

import jax
import jax.numpy as jnp
from jax import lax
from jax.experimental import pallas as pl
from jax.experimental.pallas import tpu as pltpu

D_MODEL = 1024
N_META = 16
DIFF_HEADS = 8
DIFF_HEAD_DIM = 64
DIFF_V_DIM = 128
SB_HEADS = 16
SB_HEAD_DIM = 64
DEPTH = 1
DN_ALPHA = (2.0 * DEPTH) ** 0.25
LN_EPS = 1e-5
RMS_EPS = 1e-5
LAM_INIT = 0.8 - 0.6 * math.exp(-0.3 * 0)

LANES = 128
TD = 512
TS = 256
SB_HP = 4
TM_PROJ = 512
TM_FINAL = 256
VMEM_LIMIT = 56 * 1024 * 1024
LOG2E = 1.4426950408889634
NEG_BIG = -1e30
SB_EXIT_LOG = -105.0
EXP2_ZERO = 150.0
NORM_SLACK = 1.01

_NT = (((1,), (1,)), ((), ()))


def _layer_norm(x, g, b):
    mu = jnp.mean(x, axis=-1, keepdims=True)
    xc = x - mu
    var = jnp.mean(xc * xc, axis=-1, keepdims=True)
    return xc * lax.rsqrt(var + LN_EPS) * g + b


def _proj_kernel(x_ref, g_ref, b_ref, w_ref, wvt_ref, qd_ref, kd_ref, qs_ref, ks_ref, vdt_ref, vst_ref):
    h = _layer_norm(x_ref[...], g_ref[...], b_ref[...]).astype(jnp.bfloat16)
    for c, o_ref in enumerate((qd_ref, kd_ref, qs_ref, ks_ref)):
        o = jnp.dot(h, w_ref[:, c * D_MODEL:(c + 1) * D_MODEL], preferred_element_type=jnp.float32)
        o_ref[...] = o.astype(jnp.bfloat16)
    for c, (o_ref, tk) in enumerate(((vdt_ref, TD), (vst_ref, TS))):
        vt = lax.dot_general(wvt_ref[c * D_MODEL:(c + 1) * D_MODEL, :], h, _NT,
                             preferred_element_type=jnp.float32).astype(jnp.bfloat16)
        for t in range(TM_PROJ // tk):
            o_ref[t] = vt[:, t * tk:(t + 1) * tk]


def _proj_call(x, g, b, w_qk, w_vt):
    B, S, D = x.shape
    nt = S // TM_PROJ
    row_spec = pl.BlockSpec((None, TM_PROJ, D), lambda bi, i: (bi, i, 0))
    vt_specs = [pl.BlockSpec((None, TM_PROJ // tk, D, tk), lambda bi, i: (bi, i, 0, 0)) for tk in (TD, TS)]
    const = lambda bi, i: (0, 0)
    return pl.pallas_call(
        _proj_kernel,
        grid=(B, nt),
        in_specs=[row_spec,
                  pl.BlockSpec((1, D), const), pl.BlockSpec((1, D), const),
                  pl.BlockSpec((D, 4 * D), const), pl.BlockSpec((2 * D, D), const)],
        out_specs=[row_spec, row_spec, row_spec, row_spec] + vt_specs,
        out_shape=[jax.ShapeDtypeStruct((B, S, D), jnp.bfloat16)] * 4
        + [jax.ShapeDtypeStruct((B, S // tk, D, tk), jnp.bfloat16) for tk in (TD, TS)],
        compiler_params=pltpu.CompilerParams(dimension_semantics=("arbitrary", "arbitrary"),
                                             vmem_limit_bytes=VMEM_LIMIT),
        name="proj",
    )(x, g, b, w_qk, w_vt)


def _meta_proj_kernel(m_ref, g_ref, b_ref, w_ref, wvt_ref, k_ref, vt_ref):
    h = _layer_norm(m_ref[...], g_ref[...], b_ref[...]).astype(jnp.bfloat16)
    k_ref[...] = jnp.dot(h, w_ref[...], preferred_element_type=jnp.float32).astype(jnp.bfloat16)
    vt_ref[...] = lax.dot_general(wvt_ref[...], h, _NT,
                                  preferred_element_type=jnp.float32).astype(jnp.bfloat16)


def _meta_proj_call(meta, g, b, w_k, w_vt):
    D = D_MODEL
    return pl.pallas_call(
        _meta_proj_kernel,
        out_shape=[jax.ShapeDtypeStruct((N_META, 2 * D), jnp.bfloat16),
                   jax.ShapeDtypeStruct((2 * D, N_META), jnp.bfloat16)],
        compiler_params=pltpu.CompilerParams(vmem_limit_bytes=VMEM_LIMIT),
        name="meta_proj",
    )(meta, g, b, w_k, w_vt)


def _diff_kernel(slope_ref, lam_ref, q_ref, k_ref, vt_ref, km_ref, vmt_ref, g_ref, o_ref,
                 s_scr, p_scr, acc_scr, m_scr, l_scr, bias_scr, qn_scr, kpre_ref):
    S = q_ref.shape[0]
    nq = S // TD
    slope2 = slope_ref[pl.program_id(1)] * LOG2E
    f32, bf16 = jnp.float32, jnp.bfloat16

    row = lax.broadcasted_iota(jnp.int32, (TD, TD), 0)
    col = lax.broadcasted_iota(jnp.int32, (TD, TD), 1)
    bias = slope2 * row.astype(f32)
    bias_scr[0] = bias
    bias_scr[1] = jnp.where(row <= col, bias, NEG_BIG)

    def split(qt):
        zero = jnp.zeros_like(qt)
        ln = lax.broadcasted_iota(jnp.int32, qt.shape, 1)
        return jnp.where(ln < DIFF_HEAD_DIM, qt, zero), jnp.where(ln >= DIFF_HEAD_DIM, qt, zero)

    colq = lax.broadcasted_iota(jnp.int32, (N_META, S), 1)
    rowm = lax.broadcasted_iota(jnp.int32, (N_META, S), 0)
    bias_m = slope2 * (rowm - N_META - (colq - (colq & (TD - 1)))).astype(f32)
    for c, qc in enumerate(split(q_ref[...])):
        s = lax.dot_general(km_ref[...], qc, _NT, preferred_element_type=f32) + bias_m
        m0 = jnp.max(s, axis=0, keepdims=True)
        p = jnp.exp2(s - m0)
        l0 = jnp.sum(p, axis=0, keepdims=True)
        a0 = jnp.dot(vmt_ref[...], p.astype(bf16), preferred_element_type=f32)
        for t in range(nq):
            m_scr[t, c] = m0[:, t * TD:(t + 1) * TD]
            l_scr[t, c] = l0[:, t * TD:(t + 1) * TD]
            acc_scr[t, c] = a0[:, t * TD:(t + 1) * TD]

    def norms(x):
        xf = x.astype(f32)
        sq = (xf * xf).astype(bf16)
        ln = lax.broadcasted_iota(jnp.int32, (8, LANES), 1)
        halves = (ln < DIFF_HEAD_DIM, ln >= DIFF_HEAD_DIM)
        return [jnp.sqrt(lax.dot_general(jnp.where(h, 1.0, 0.0).astype(bf16), sq, _NT,
                                         preferred_element_type=f32)[0:1, :]) for h in halves]

    for c, (qn, kn) in enumerate(zip(norms(q_ref[...]), norms(k_ref[...]))):
        kmax = jnp.float32(0.0)
        for t in range(nq):
            qn_scr[t, c] = qn[:, t * TD:(t + 1) * TD]
            kmax = jnp.maximum(kmax, jnp.max(kn[:, t * TD:(t + 1) * TD]))
            kpre_ref[c * nq + t] = kmax

    def stage_a(qt, kt):
        k_tile = k_ref[pl.ds(pl.multiple_of(kt * TD, TD), TD), :]
        qcs = split(q_ref[pl.ds(pl.multiple_of(qt * TD, TD), TD), :])
        tile_bias = bias_scr[(kt == qt).astype(jnp.int32)]
        mx = []
        for c in range(2):
            s = lax.dot_general(k_tile, qcs[c], _NT, preferred_element_type=f32) + tile_bias
            s_scr[c] = s
            mx.append(jnp.max(s, axis=0, keepdims=True))
        return tuple(mx)

    def stage_b(qt, kt, mx):
        off2 = -(slope2 * TD) * (qt - kt).astype(f32)
        alphas, m_news = [], []
        for c in range(2):
            m_old = m_scr[qt, c]
            m_new = jnp.maximum(m_old, mx[c] + off2)
            alpha = jnp.exp2(m_old - m_new)
            p = jnp.exp2(s_scr[c] - (m_new - off2))
            l_scr[qt, c] = alpha * l_scr[qt, c] + jnp.sum(p, axis=0, keepdims=True)
            m_scr[qt, c] = m_new
            p_scr[c] = p.astype(bf16)
            alphas.append(alpha)
            m_news.append(m_new)
        return tuple(alphas), tuple(m_news)

    def rest_is_zero(qt, kt, m_news):
        worst_bias = slope2 * (TD - 1) - (slope2 * TD) * (qt - kt).astype(f32)
        slack = None
        for c in range(2):
            bound = qn_scr[qt, c] * (kpre_ref[c * nq + kt] * NORM_SLACK) + (worst_bias + 1.0)
            gap = (m_news[c] - EXP2_ZERO) - bound
            slack = gap if slack is None else jnp.minimum(slack, gap)
        return jnp.min(slack) > 0.0

    def stage_c(qt, kt, alphas):
        vt_tile = vt_ref[kt]
        for c in range(2):
            acc_scr[qt, c] = alphas[c] * acc_scr[qt, c] + jnp.dot(vt_tile, p_scr[c], preferred_element_type=f32)

    lam = (jnp.exp(jnp.sum(lam_ref[0:1, :] * lam_ref[1:2, :], axis=1, keepdims=True))
           - jnp.exp(jnp.sum(lam_ref[2:3, :] * lam_ref[3:4, :], axis=1, keepdims=True)) + LAM_INIT)

    def finalize(qt):
        y = (acc_scr[qt, 0] * (1.0 / l_scr[qt, 0]) - lam * (acc_scr[qt, 1] * (1.0 / l_scr[qt, 1])))
        ms = jnp.mean(y * y, axis=0, keepdims=True)
        y = y * lax.rsqrt(ms + RMS_EPS) * g_ref[...] * (1.0 - LAM_INIT)
        o_ref[pl.ds(pl.multiple_of(qt * TD, TD), TD), :] = y.T.astype(o_ref.dtype)

    p_scr[...] = jnp.zeros(p_scr.shape, bf16)
    ones = jnp.ones((1, TD), f32)
    zero_i = jnp.int32(0)
    mx0 = stage_a(zero_i, zero_i)

    def cond(st):
        return st[2] < nq

    def body(st):
        qa, ka, qb, kb, eb, qc, kc, ec, mx, alphas = st
        stage_c(qc, kc, alphas)
        alphas, m_news = stage_b(qb, kb, mx)
        qa_c, ka_c = jnp.minimum(qa, nq - 1), jnp.minimum(ka, nq - 1)
        ea = jnp.logical_or(ka == 0, jnp.logical_and(qa == qb, rest_is_zero(qa_c, jnp.maximum(ka_c - 1, 0), m_news)))
        mx = stage_a(qa_c, ka_c)

        @pl.when(ec)
        def _():
            finalize(qc)

        return (jnp.where(ea, qa + 1, qa), jnp.where(ea, qa + 1, ka - 1), qa, ka, ea, qb, kb, eb, mx, alphas)

    st = (jnp.int32(1), jnp.int32(1), zero_i, zero_i, jnp.bool_(True), zero_i, zero_i, jnp.bool_(False),
          mx0, (ones, ones))
    st = lax.while_loop(cond, body, st)
    _, _, _, _, _, qc, kc, _, _, alphas = st
    stage_c(qc, kc, alphas)
    finalize(qc)


def _diff_call(slopes, lam, qd, kd, vdt, kdm, vdmt, g_col):
    B, S, D = qd.shape
    nq = S // TD
    smem = pl.BlockSpec(memory_space=pltpu.SMEM)
    slab = pl.BlockSpec((None, S, LANES), lambda b, h: (b, 0, h))
    return pl.pallas_call(
        _diff_kernel,
        grid=(B, DIFF_HEADS),
        in_specs=[smem, pl.BlockSpec((4, DIFF_HEAD_DIM), lambda b, h: (0, 0)),
                  slab, slab,
                  pl.BlockSpec((None, nq, DIFF_V_DIM, TD), lambda b, h: (b, 0, h, 0)),
                  pl.BlockSpec((N_META, LANES), lambda b, h: (0, h)),
                  pl.BlockSpec((DIFF_V_DIM, N_META), lambda b, h: (h, 0)),
                  pl.BlockSpec((DIFF_V_DIM, 1), lambda b, h: (0, 0))],
        out_specs=slab,
        out_shape=jax.ShapeDtypeStruct((B, S, D), jnp.bfloat16),
        scratch_shapes=[pltpu.VMEM((2, TD, TD), jnp.float32),
                        pltpu.VMEM((2, TD, TD), jnp.bfloat16),
                        pltpu.VMEM((nq, 2, DIFF_V_DIM, TD), jnp.float32),
                        pltpu.VMEM((nq, 2, 1, TD), jnp.float32),
                        pltpu.VMEM((nq, 2, 1, TD), jnp.float32),
                        pltpu.VMEM((2, TD, TD), jnp.float32),
                        pltpu.VMEM((nq, 2, 1, TD), jnp.float32),
                        pltpu.SMEM((2 * nq,), jnp.float32)],
        compiler_params=pltpu.CompilerParams(dimension_semantics=("arbitrary",) * 2,
                                             vmem_limit_bytes=VMEM_LIMIT),
        name="diff_attn",
    )(slopes, lam, qd, kd, vdt, kdm, vdmt, g_col)


def _sb_kernel(q_ref, k_ref, vt_ref, km_ref, vmt_ref, u_ref, um_ref, o_ref):
    qi = pl.program_id(2)
    q = q_ref[...]
    lane = lax.broadcasted_iota(jnp.int32, q.shape, 1)
    zero = jnp.zeros_like(q)
    row = lax.broadcasted_iota(jnp.int32, (TS, TS), 0)
    col = lax.broadcasted_iota(jnp.int32, (TS, TS), 1)
    strict = row < col

    def logits(k_tile, qh):
        return lax.dot_general(k_tile, qh, _NT, preferred_element_type=jnp.float32)

    def split_softplus(z, mask):
        e = jnp.exp2(jnp.abs(z) * (-LOG2E))
        sp = jnp.maximum(z, 0.0) + jnp.log(1.0 + e)
        if mask is not None:
            sp = jnp.where(mask, sp, 0.0)
        hi = sp.astype(jnp.bfloat16)
        return hi, (sp - hi.astype(jnp.float32)).astype(jnp.bfloat16)

    def suffix(u, hi, lo):
        return jnp.dot(u, hi, preferred_element_type=jnp.float32) + jnp.dot(u, lo, preferred_element_type=jnp.float32)

    def weights(z, incl, carry, mask):
        a = jnp.exp(z + incl + carry)
        if mask is not None:
            a = jnp.where(mask, a, 0.0)
        return a.astype(jnp.bfloat16)

    def tile(k_tile, v_tile, u, qh, carry, acc):
        z = logits(k_tile, qh)
        incl = suffix(u, *split_softplus(z, None))
        pv = jnp.dot(v_tile, weights(z, incl, carry, None), preferred_element_type=jnp.float32)
        return carry + incl[0:1, :], acc + pv

    has_left = qi > 0
    jl = jnp.maximum(qi - 1, 0)
    k0 = pl.multiple_of(qi * TS, TS)
    kl = pl.multiple_of(jl * TS, TS)
    u = u_ref[...]
    qhs = [jnp.where(jnp.logical_and(lane >= hh * SB_HEAD_DIM, lane < (hh + 1) * SB_HEAD_DIM), q, zero)
           for hh in range(SB_HP)]

    chains = [(hh, strict, k0, qi) for hh in range(SB_HP)] + [(hh, None, kl, jl) for hh in range(SB_HP)]
    zs = [logits(k_ref[pl.ds(kk, TS), :], qhs[hh]) for hh, _, kk, _ in chains]
    parts = [split_softplus(z, mask) for z, (_, mask, _, _) in zip(zs, chains)]
    incls = [suffix(u, hi, lo) for hi, lo in parts]
    tots = [incl[0:1, :] for incl in incls]
    carries = [0.0] * SB_HP + [jnp.where(has_left, tots[hh], NEG_BIG) for hh in range(SB_HP)]
    ws = [weights(z, incl, c, mask) for z, incl, c, (_, mask, _, _) in zip(zs, incls, carries, chains)]
    pvs = [jnp.dot(vt_ref[j, hh * SB_HEAD_DIM:(hh + 1) * SB_HEAD_DIM, :], w, preferred_element_type=jnp.float32)
           for w, (hh, _, _, j) in zip(ws, chains)]
    near = [(qhs[hh], hh * SB_HEAD_DIM, tots[hh] + jnp.where(has_left, tots[SB_HP + hh], 0.0),
             pvs[hh] + pvs[SB_HP + hh]) for hh in range(SB_HP)]

    def further_tiles(accs):
        outs = []
        for (qh, r0, carry, _), acc in zip(near, accs):

            def cond(st):
                j, cmax, _, _ = st
                return jnp.logical_and(j >= 0, cmax > SB_EXIT_LOG)

            def body(st):
                j, _, carry, acc = st
                kk = pl.multiple_of(j * TS, TS)
                carry, acc = tile(k_ref[pl.ds(kk, TS), :], vt_ref[j, r0:r0 + SB_HEAD_DIM, :], u, qh, carry, acc)
                return j - 1, jnp.max(carry), carry, acc

            _, cmax, carry, acc = lax.while_loop(cond, body, (qi - 2, jnp.max(carry), carry, acc))

            def meta(acc):
                return tile(km_ref[...], vmt_ref[r0:r0 + SB_HEAD_DIM, :], um_ref[...], qh, carry, acc)[1]

            outs.append(lax.cond(cmax > SB_EXIT_LOG, meta, lambda a: a, acc))
        return tuple(outs)

    worst = functools.reduce(jnp.maximum, [carry for _, _, carry, _ in near])
    accs = lax.cond(jnp.max(worst) > SB_EXIT_LOG, further_tiles, lambda a: a, tuple(acc for _, _, _, acc in near))

    o_ref[...] = jnp.concatenate(accs, axis=0).T.astype(o_ref.dtype)


def _sb_call(qs, ks, vst, ksm, vsmt, u, um):
    B, S, D = qs.shape
    nq = S // TS
    width = SB_HP * SB_HEAD_DIM
    const = lambda b, h, i: (0, 0)
    return pl.pallas_call(
        _sb_kernel,
        grid=(B, SB_HEADS // SB_HP, nq),
        in_specs=[pl.BlockSpec((None, TS, width), lambda b, h, i: (b, i, h)),
                  pl.BlockSpec((None, S, width), lambda b, h, i: (b, 0, h)),
                  pl.BlockSpec((None, S // TS, width, TS), lambda b, h, i: (b, 0, h, 0)),
                  pl.BlockSpec((N_META, width), lambda b, h, i: (0, h)),
                  pl.BlockSpec((width, N_META), lambda b, h, i: (h, 0)),
                  pl.BlockSpec((TS, TS), const), pl.BlockSpec((N_META, N_META), const)],
        out_specs=pl.BlockSpec((None, TS, width), lambda b, h, i: (b, i, h)),
        out_shape=jax.ShapeDtypeStruct((B, S, D), jnp.bfloat16),
        compiler_params=pltpu.CompilerParams(dimension_semantics=("arbitrary",) * 3,
                                             vmem_limit_bytes=VMEM_LIMIT),
        name="sb_attn",
    )(qs, ks, vst, ksm, vsmt, u, um)


def _sigmoid(x):
    return 1.0 / (1.0 + jnp.exp(-x))


def _final_kernel(x_ref, yd_ref, ys_ref, eg_ref, eb_ref, wg_ref, wbd_ref, wbs_ref, wo_ref, bg_ref,
                  lg_ref, lb_ref, o_ref):
    D = D_MODEL
    h = _layer_norm(x_ref[...], eg_ref[...], eb_ref[...])
    hb = h.astype(jnp.bfloat16)

    def gate(c):
        return jnp.dot(hb, wg_ref[:, c * D:(c + 1) * D], preferred_element_type=jnp.float32)

    g_d = gate(0)
    u_d = (yd_ref[...].astype(jnp.float32) * (g_d * _sigmoid(g_d))).astype(jnp.bfloat16)
    br_d = jnp.dot(u_d, wbd_ref[...], preferred_element_type=jnp.float32)
    merged = _sigmoid(gate(2) + bg_ref[0:1, :]) * br_d
    g_s = gate(1)
    u_s = (ys_ref[...].astype(jnp.float32) * (g_s * _sigmoid(g_s))).astype(jnp.bfloat16)
    br_s = jnp.dot(u_s, wbs_ref[...], preferred_element_type=jnp.float32)
    merged = merged + _sigmoid(gate(3) + bg_ref[1:2, :]) * br_s
    y = jnp.dot(merged.astype(jnp.bfloat16), wo_ref[...], preferred_element_type=jnp.float32)
    o_ref[...] = _layer_norm(DN_ALPHA * h + y, lg_ref[...], lb_ref[...]).astype(o_ref.dtype)


def _final_call(x2, yd2, ys2, eg, eb, wg, wbd, wbs, wo, bg, lg, lb):
    R, D = x2.shape
    row = pl.BlockSpec((TM_FINAL, D), lambda i: (i, 0))
    const = lambda i: (0, 0)
    vec = pl.BlockSpec((1, D), const)
    return pl.pallas_call(
        _final_kernel,
        grid=(R // TM_FINAL,),
        in_specs=[row, row, row, vec, vec,
                  pl.BlockSpec((D, 4 * D), const), pl.BlockSpec((D, D), const),
                  pl.BlockSpec((D, D), const), pl.BlockSpec((D, D), const),
                  pl.BlockSpec((2, D), const), vec, vec],
        out_specs=row,
        out_shape=jax.ShapeDtypeStruct((R, D), jnp.float32),
        compiler_params=pltpu.CompilerParams(dimension_semantics=("arbitrary",),
                                             vmem_limit_bytes=VMEM_LIMIT),
        name="final",
    )(x2, yd2, ys2, eg, eb, wg, wbd, wbs, wo, bg, lg, lb)


def _neg_upper_ones(n):
    r = jnp.arange(n)
    return jnp.where(r[None, :] >= r[:, None], -1.0, 0.0).astype(jnp.bfloat16)


def kernel(x, meta_tokens, emb_ln_g, emb_ln_b, w_in, b_gate, diff_lambda, diff_subln_g,
           w_br_diff, w_br_sb, w_out, ln_g, ln_b):
    B, S, D = x.shape
    bf16 = jnp.bfloat16
    w = w_in[0]
    col = lambda i: w[:, i * D:(i + 1) * D]
    qscale = DIFF_HEAD_DIM ** -0.5
    w_qk = jnp.concatenate([col(0) * (qscale * LOG2E), col(1), col(4) * qscale, col(5)], axis=1).astype(bf16)
    w_vt = jnp.concatenate([col(2).T, col(6).T], axis=0).astype(bf16)
    w_kmeta = jnp.concatenate([col(1), col(5)], axis=1).astype(bf16)
    w_gate = jnp.concatenate([col(3), col(7), col(8), col(9)], axis=1).astype(bf16)
    eg, eb = emb_ln_g.reshape(1, D), emb_ln_b.reshape(1, D)

    qd, kd, qs, ks, vdt, vst = _proj_call(x, eg, eb, w_qk, w_vt)
    k_meta, vt_meta = _meta_proj_call(meta_tokens, eg, eb, w_kmeta, w_vt)

    slopes = 2.0 ** (-8.0 * jnp.arange(1, DIFF_HEADS + 1, dtype=jnp.float32) / DIFF_HEADS)

    yd = _diff_call(slopes, diff_lambda[0], qd, kd, vdt, k_meta[:, :D], vt_meta[:D], diff_subln_g[0].reshape(DIFF_V_DIM, 1))
    ys = _sb_call(qs, ks, vst, k_meta[:, D:], vt_meta[D:], _neg_upper_ones(TS), _neg_upper_ones(N_META))

    out = _final_call(x.reshape(B * S, D), yd.reshape(B * S, D), ys.reshape(B * S, D), eg, eb,
                      w_gate, w_br_diff[0].astype(bf16), w_br_sb[0].astype(bf16), w_out[0].astype(bf16),
                      b_gate[0], ln_g[0].reshape(1, D), ln_b[0].reshape(1, D))
    return out.reshape(B, S, D)
```

```python
import functools
import math

import jax
import jax.numpy as jnp
from jax import lax
from jax.experimental import pallas as pl
from jax.experimental.pallas import tpu as pltpu

D_MODEL = 1024
N_META = 16
DIFF_HEADS = 8
DIFF_HEAD_DIM = 64
DIFF_V_DIM = 128
SB_HEADS = 16
SB_HEAD_DIM = 64
DEPTH = 1
DN_ALPHA = (2.0 * DEPTH) ** 0.25
LN_EPS = 1e-5
RMS_EPS = 1e-5
LAM_INIT = 0.8 - 0.6 * math.exp(-0.3 * 0)

LANES = 128
TD = 512
TS = 256
SB_HP = 4
TM_PROJ = 512
TM_FINAL = 256
VMEM_LIMIT = 56 * 1024 * 1024
LOG2E = 1.4426950408889634
NEG_BIG = -1e30
SB_EXIT_LOG = -105.0
EXP2_ZERO = 150.0
NORM_SLACK = 1.01

_NT = (((1,), (1,)), ((), ()))


def _layer_norm(x, g, b):
    mu = jnp.mean(x, axis=-1, keepdims=True)
    xc = x - mu
    var = jnp.mean(xc * xc, axis=-1, keepdims=True)
    return xc * lax.rsqrt(var + LN_EPS) * g + b


def _proj_kernel(x_ref, g_ref, b_ref, w_ref, wvt_ref, qd_ref, kd_ref, qs_ref, ks_ref, vdt_ref, vst_ref):
    h = _layer_norm(x_ref[...], g_ref[...], b_ref[...]).astype(jnp.bfloat16)
    for c, o_ref in enumerate((qd_ref, kd_ref, qs_ref, ks_ref)):
        o = jnp.dot(h, w_ref[:, c * D_MODEL:(c + 1) * D_MODEL], preferred_element_type=jnp.float32)
        o_ref[...] = o.astype(jnp.bfloat16)
    for c, (o_ref, tk) in enumerate(((vdt_ref, TD), (vst_ref, TS))):
        vt = lax.dot_general(wvt_ref[c * D_MODEL:(c + 1) * D_MODEL, :], h, _NT,
                             preferred_element_type=jnp.float32).astype(jnp.bfloat16)
        for t in range(TM_PROJ // tk):
            o_ref[t] = vt[:, t * tk:(t + 1) * tk]


def _proj_call(x, g, b, w_qk, w_vt):
    B, S, D = x.shape
    nt = S // TM_PROJ
    row_spec = pl.BlockSpec((None, TM_PROJ, D), lambda bi, i: (bi, i, 0))
    vt_specs = [pl.BlockSpec((None, TM_PROJ // tk, D, tk), lambda bi, i: (bi, i, 0, 0)) for tk in (TD, TS)]
    const = lambda bi, i: (0, 0)
    return pl.pallas_call(
        _proj_kernel,
        grid=(B, nt),
        in_specs=[row_spec,
                  pl.BlockSpec((1, D), const), pl.BlockSpec((1, D), const),
                  pl.BlockSpec((D, 4 * D), const), pl.BlockSpec((2 * D, D), const)],
        out_specs=[row_spec, row_spec, row_spec, row_spec] + vt_specs,
        out_shape=[jax.ShapeDtypeStruct((B, S, D), jnp.bfloat16)] * 4
        + [jax.ShapeDtypeStruct((B, S // tk, D, tk), jnp.bfloat16) for tk in (TD, TS)],
        compiler_params=pltpu.CompilerParams(dimension_semantics=("arbitrary", "arbitrary"),
                                             vmem_limit_bytes=VMEM_LIMIT),
        name="proj",
    )(x, g, b, w_qk, w_vt)


def _meta_proj_kernel(m_ref, g_ref, b_ref, w_ref, wvt_ref, k_ref, vt_ref):
    h = _layer_norm(m_ref[...], g_ref[...], b_ref[...]).astype(jnp.bfloat16)
    k_ref[...] = jnp.dot(h, w_ref[...], preferred_element_type=jnp.float32).astype(jnp.bfloat16)
    vt_ref[...] = lax.dot_general(wvt_ref[...], h, _NT,
                                  preferred_element_type=jnp.float32).astype(jnp.bfloat16)


def _meta_proj_call(meta, g, b, w_k, w_vt):
    D = D_MODEL
    return pl.pallas_call(
        _meta_proj_kernel,
        out_shape=[jax.ShapeDtypeStruct((N_META, 2 * D), jnp.bfloat16),
                   jax.ShapeDtypeStruct((2 * D, N_META), jnp.bfloat16)],
        compiler_params=pltpu.CompilerParams(vmem_limit_bytes=VMEM_LIMIT),
        name="meta_proj",
    )(meta, g, b, w_k, w_vt)


def _diff_kernel(slope_ref, lam_ref, q_ref, k_ref, vt_ref, km_ref, vmt_ref, g_ref, o_ref,
                 s_scr, p_scr, acc_scr, m_scr, l_scr, bias_scr, qn_scr, kpre_ref):
    S = q_ref.shape[0]
    nq = S // TD
    slope2 = slope_ref[pl.program_id(1)] * LOG2E
    f32, bf16 = jnp.float32, jnp.bfloat16

    row = lax.broadcasted_iota(jnp.int32, (TD, TD), 0)
    col = lax.broadcasted_iota(jnp.int32, (TD, TD), 1)
    bias = slope2 * row.astype(f32)
    bias_scr[0] = bias
    bias_scr[1] = jnp.where(row <= col, bias, NEG_BIG)

    def split(qt):
        zero = jnp.zeros_like(qt)
        ln = lax.broadcasted_iota(jnp.int32, qt.shape, 1)
        return jnp.where(ln < DIFF_HEAD_DIM, qt, zero), jnp.where(ln >= DIFF_HEAD_DIM, qt, zero)

    colq = lax.broadcasted_iota(jnp.int32, (N_META, S), 1)
    rowm = lax.broadcasted_iota(jnp.int32, (N_META, S), 0)
    bias_m = slope2 * (rowm - N_META - (colq - (colq & (TD - 1)))).astype(f32)
    for c, qc in enumerate(split(q_ref[...])):
        s = lax.dot_general(km_ref[...], qc, _NT, preferred_element_type=f32) + bias_m
        m0 = jnp.max(s, axis=0, keepdims=True)
        p = jnp.exp2(s - m0)
        l0 = jnp.sum(p, axis=0, keepdims=True)
        a0 = jnp.dot(vmt_ref[...], p.astype(bf16), preferred_element_type=f32)
        for t in range(nq):
            m_scr[t, c] = m0[:, t * TD:(t + 1) * TD]
            l_scr[t, c] = l0[:, t * TD:(t + 1) * TD]
            acc_scr[t, c] = a0[:, t * TD:(t + 1) * TD]

    def norms(x):
        xf = x.astype(f32)
        sq = (xf * xf).astype(bf16)
        ln = lax.broadcasted_iota(jnp.int32, (8, LANES), 1)
        halves = (ln < DIFF_HEAD_DIM, ln >= DIFF_HEAD_DIM)
        return [jnp.sqrt(lax.dot_general(jnp.where(h, 1.0, 0.0).astype(bf16), sq, _NT,
                                         preferred_element_type=f32)[0:1, :]) for h in halves]

    for c, (qn, kn) in enumerate(zip(norms(q_ref[...]), norms(k_ref[...]))):
        kmax = jnp.float32(0.0)
        for t in range(nq):
            qn_scr[t, c] = qn[:, t * TD:(t + 1) * TD]
            kmax = jnp.maximum(kmax, jnp.max(kn[:, t * TD:(t + 1) * TD]))
            kpre_ref[c * nq + t] = kmax

    def stage_a(qt, kt):
        k_tile = k_ref[pl.ds(pl.multiple_of(kt * TD, TD), TD), :]
        qcs = split(q_ref[pl.ds(pl.multiple_of(qt * TD, TD), TD), :])
        tile_bias = bias_scr[(kt == qt).astype(jnp.int32)]
        mx = []
        for c in range(2):
            s = lax.dot_general(k_tile, qcs[c], _NT, preferred_element_type=f32) + tile_bias
            s_scr[c] = s
            mx.append(jnp.max(s, axis=0, keepdims=True))
        return tuple(mx)

    def stage_b(qt, kt, mx):
        off2 = -(slope2 * TD) * (qt - kt).astype(f32)
        alphas, m_news = [], []
        for c in range(2):
            m_old = m_scr[qt, c]
            m_new = jnp.maximum(m_old, mx[c] + off2)
            alpha = jnp.exp2(m_old - m_new)
            p = jnp.exp2(s_scr[c] - (m_new - off2))
            l_scr[qt, c] = alpha * l_scr[qt, c] + jnp.sum(p, axis=0, keepdims=True)
            m_scr[qt, c] = m_new
            p_scr[c] = p.astype(bf16)
            alphas.append(alpha)
            m_news.append(m_new)
        return tuple(alphas), tuple(m_news)

    def rest_is_zero(qt, kt, m_news):
        worst_bias = slope2 * (TD - 1) - (slope2 * TD) * (qt - kt).astype(f32)
        slack = None
        for c in range(2):
            bound = qn_scr[qt, c] * (kpre_ref[c * nq + kt] * NORM_SLACK) + (worst_bias + 1.0)
            gap = (m_news[c] - EXP2_ZERO) - bound
            slack = gap if slack is None else jnp.minimum(slack, gap)
        return jnp.min(slack) > 0.0

    def stage_c(qt, kt, alphas):
        vt_tile = vt_ref[kt]
        for c in range(2):
            acc_scr[qt, c] = alphas[c] * acc_scr[qt, c] + jnp.dot(vt_tile, p_scr[c], preferred_element_type=f32)

    lam = (jnp.exp(jnp.sum(lam_ref[0:1, :] * lam_ref[1:2, :], axis=1, keepdims=True))
           - jnp.exp(jnp.sum(lam_ref[2:3, :] * lam_ref[3:4, :], axis=1, keepdims=True)) + LAM_INIT)

    def finalize(qt):
        y = (acc_scr[qt, 0] * (1.0 / l_scr[qt, 0]) - lam * (acc_scr[qt, 1] * (1.0 / l_scr[qt, 1])))
        ms = jnp.mean(y * y, axis=0, keepdims=True)
        y = y * lax.rsqrt(ms + RMS_EPS) * g_ref[...] * (1.0 - LAM_INIT)
        o_ref[pl.ds(pl.multiple_of(qt * TD, TD), TD), :] = y.T.astype(o_ref.dtype)

    p_scr[...] = jnp.zeros(p_scr.shape, bf16)
    ones = jnp.ones((1, TD), f32)
    zero_i = jnp.int32(0)
    mx0 = stage_a(zero_i, zero_i)

    def cond(st):
        return st[2] < nq

    def body(st):
        qa, ka, qb, kb, eb, qc, kc, ec, mx, alphas = st
        stage_c(qc, kc, alphas)
        alphas, m_news = stage_b(qb, kb, mx)
        qa_c, ka_c = jnp.minimum(qa, nq - 1), jnp.minimum(ka, nq - 1)
        ea = jnp.logical_or(ka == 0, jnp.logical_and(qa == qb, rest_is_zero(qa_c, jnp.maximum(ka_c - 1, 0), m_news)))
        mx = stage_a(qa_c, ka_c)

        @pl.when(ec)
        def _():
            finalize(qc)

        return (jnp.where(ea, qa + 1, qa), jnp.where(ea, qa + 1, ka - 1), qa, ka, ea, qb, kb, eb, mx, alphas)

    st = (jnp.int32(1), jnp.int32(1), zero_i, zero_i, jnp.bool_(True), zero_i, zero_i, jnp.bool_(False),
          mx0, (ones, ones))
    st = lax.while_loop(cond, body, st)
    _, _, _, _, _, qc, kc, _, _, alphas = st
    stage_c(qc, kc, alphas)
    finalize(qc)


def _diff_call(slopes, lam, qd, kd, vdt, kdm, vdmt, g_col):
    B, S, D = qd.shape
    nq = S // TD
    smem = pl.BlockSpec(memory_space=pltpu.SMEM)
    slab = pl.BlockSpec((None, S, LANES), lambda b, h: (b, 0, h))
    return pl.pallas_call(
        _diff_kernel,
        grid=(B, DIFF_HEADS),
        in_specs=[smem, pl.BlockSpec((4, DIFF_HEAD_DIM), lambda b, h: (0, 0)),
                  slab, slab,
                  pl.BlockSpec((None, nq, DIFF_V_DIM, TD), lambda b, h: (b, 0, h, 0)),
                  pl.BlockSpec((N_META, LANES), lambda b, h: (0, h)),
                  pl.BlockSpec((DIFF_V_DIM, N_META), lambda b, h: (h, 0)),
                  pl.BlockSpec((DIFF_V_DIM, 1), lambda b, h: (0, 0))],
        out_specs=slab,
        out_shape=jax.ShapeDtypeStruct((B, S, D), jnp.bfloat16),
        scratch_shapes=[pltpu.VMEM((2, TD, TD), jnp.float32),
                        pltpu.VMEM((2, TD, TD), jnp.bfloat16),
                        pltpu.VMEM((nq, 2, DIFF_V_DIM, TD), jnp.float32),
                        pltpu.VMEM((nq, 2, 1, TD), jnp.float32),
                        pltpu.VMEM((nq, 2, 1, TD), jnp.float32),
                        pltpu.VMEM((2, TD, TD), jnp.float32),
                        pltpu.VMEM((nq, 2, 1, TD), jnp.float32),
                        pltpu.SMEM((2 * nq,), jnp.float32)],
        compiler_params=pltpu.CompilerParams(dimension_semantics=("arbitrary",) * 2,
                                             vmem_limit_bytes=VMEM_LIMIT),
        name="diff_attn",
    )(slopes, lam, qd, kd, vdt, kdm, vdmt, g_col)


def _sb_kernel(q_ref, k_ref, vt_ref, km_ref, vmt_ref, u_ref, um_ref, o_ref, zi_scr, out_scr):
    S = q_ref.shape[0]
    nq = S // TS
    f32, bf16 = jnp.float32, jnp.bfloat16
    u = u_ref[...]
    row = lax.broadcasted_iota(jnp.int32, (TS, TS), 0)
    col = lax.broadcasted_iota(jnp.int32, (TS, TS), 1)
    strict = row < col

    def rows(i):
        return pl.ds(pl.multiple_of(i * TS, TS), TS)

    def head_queries(i):
        q = q_ref[rows(i), :]
        lane = lax.broadcasted_iota(jnp.int32, q.shape, 1)
        zero = jnp.zeros_like(q)
        return [jnp.where(jnp.logical_and(lane >= hh * SB_HEAD_DIM, lane < (hh + 1) * SB_HEAD_DIM), q, zero)
                for hh in range(SB_HP)]

    def head_rows(hh):
        return slice(hh * SB_HEAD_DIM, (hh + 1) * SB_HEAD_DIM)

    def suffix_logits(k_tile, qh, mask, umat):
        z = lax.dot_general(k_tile, qh, _NT, preferred_element_type=f32)
        e = jnp.exp2(jnp.abs(z) * (-LOG2E))
        sp = jnp.maximum(z, 0.0) + jnp.log(1.0 + e)
        if mask is not None:
            sp = jnp.where(mask, sp, 0.0)
        incl = jnp.dot(umat, sp.astype(bf16), preferred_element_type=f32)
        return z + incl, incl[0:1, :]

    def stage_a(i):
        qhs = head_queries(i)
        il = jnp.maximum(i - 1, 0)
        k_d, k_l = k_ref[rows(i), :], k_ref[rows(il), :]
        tots = []
        for ch in range(2 * SB_HP):
            diag = ch < SB_HP
            zi, tot = suffix_logits(k_d if diag else k_l, qhs[ch % SB_HP], strict if diag else None, u)
            zi_scr[ch] = zi
            tots.append(tot)
        return tuple(tots)

    def stage_b(i, tots):
        has_left = i > 0
        il = jnp.maximum(i - 1, 0)
        carries = []
        for hh in range(SB_HP):
            w_d = jnp.where(strict, jnp.exp(zi_scr[hh]), 0.0).astype(bf16)
            w_l = jnp.exp(zi_scr[SB_HP + hh] + jnp.where(has_left, tots[hh], NEG_BIG)).astype(bf16)
            out_scr[head_rows(hh), :] = (jnp.dot(vt_ref[i, head_rows(hh), :], w_d, preferred_element_type=f32)
                                         + jnp.dot(vt_ref[il, head_rows(hh), :], w_l, preferred_element_type=f32))
            carries.append(tots[hh] + jnp.where(has_left, tots[SB_HP + hh], 0.0))
        return carries

    def further_tiles(i, carries):
        qhs = head_queries(i)
        for hh in range(SB_HP):

            def tile(k_tile, v_tile, umat, carry, acc):
                zi, tot = suffix_logits(k_tile, qhs[hh], None, umat)
                w = jnp.exp(zi + carry).astype(bf16)
                return carry + tot, acc + jnp.dot(v_tile, w, preferred_element_type=f32)

            def cond(st):
                j, cmax, _, _ = st
                return jnp.logical_and(j >= 0, cmax > SB_EXIT_LOG)

            def body(st):
                j, _, carry, acc = st
                carry, acc = tile(k_ref[rows(j), :], vt_ref[j, head_rows(hh), :], u, carry, acc)
                return j - 1, jnp.max(carry), carry, acc

            st = (i - 2, jnp.max(carries[hh]), carries[hh], out_scr[head_rows(hh), :])
            _, cmax, carry, acc = lax.while_loop(cond, body, st)

            def meta(acc):
                return tile(km_ref[...], vmt_ref[head_rows(hh), :], um_ref[...], carry, acc)[1]

            out_scr[head_rows(hh), :] = lax.cond(cmax > SB_EXIT_LOG, meta, lambda a: a, acc)

    def write_out(i):
        o_ref[rows(i), :] = out_scr[...].T.astype(o_ref.dtype)

    zi_scr[...] = jnp.zeros(zi_scr.shape, f32)
    out_scr[...] = jnp.zeros(out_scr.shape, f32)
    zero_row = jnp.zeros((1, TS), f32)

    def body(i, tots):
        ib = jnp.maximum(i - 1, 0)
        write_out(jnp.maximum(i - 2, 0))
        carries = stage_b(ib, tots)
        tots = stage_a(jnp.minimum(i, nq - 1))
        worst = functools.reduce(jnp.maximum, carries)

        @pl.when(jnp.logical_and(i >= 1, jnp.max(worst) > SB_EXIT_LOG))
        def _():
            further_tiles(ib, carries)

        return tots

    lax.fori_loop(0, nq + 1, body, (zero_row,) * (2 * SB_HP))
    write_out(nq - 1)


def _sb_call(qs, ks, vst, ksm, vsmt, u, um):
    B, S, D = qs.shape
    width = SB_HP * SB_HEAD_DIM
    const = lambda b, h: (0, 0)
    slab = pl.BlockSpec((None, S, width), lambda b, h: (b, 0, h))
    return pl.pallas_call(
        _sb_kernel,
        grid=(B, SB_HEADS // SB_HP),
        in_specs=[slab, slab,
                  pl.BlockSpec((None, S // TS, width, TS), lambda b, h: (b, 0, h, 0)),
                  pl.BlockSpec((N_META, width), lambda b, h: (0, h)),
                  pl.BlockSpec((width, N_META), lambda b, h: (h, 0)),
                  pl.BlockSpec((TS, TS), const), pl.BlockSpec((N_META, N_META), const)],
        out_specs=slab,
        out_shape=jax.ShapeDtypeStruct((B, S, D), jnp.bfloat16),
        scratch_shapes=[pltpu.VMEM((2 * SB_HP, TS, TS), jnp.float32),
                        pltpu.VMEM((width, TS), jnp.float32)],
        compiler_params=pltpu.CompilerParams(dimension_semantics=("arbitrary",) * 2,
                                             vmem_limit_bytes=VMEM_LIMIT),
        name="sb_attn",
    )(qs, ks, vst, ksm, vsmt, u, um)


def _sigmoid(x):
    return 1.0 / (1.0 + jnp.exp(-x))


def _final_kernel(x_ref, yd_ref, ys_ref, eg_ref, eb_ref, wg_ref, wbd_ref, wbs_ref, wo_ref, bg_ref,
                  lg_ref, lb_ref, o_ref):
    D = D_MODEL
    h = _layer_norm(x_ref[...], eg_ref[...], eb_ref[...])
    hb = h.astype(jnp.bfloat16)

    def gate(c):
        return jnp.dot(hb, wg_ref[:, c * D:(c + 1) * D], preferred_element_type=jnp.float32)

    g_d = gate(0)
    u_d = (yd_ref[...].astype(jnp.float32) * (g_d * _sigmoid(g_d))).astype(jnp.bfloat16)
    br_d = jnp.dot(u_d, wbd_ref[...], preferred_element_type=jnp.float32)
    merged = _sigmoid(gate(2) + bg_ref[0:1, :]) * br_d
    g_s = gate(1)
    u_s = (ys_ref[...].astype(jnp.float32) * (g_s * _sigmoid(g_s))).astype(jnp.bfloat16)
    br_s = jnp.dot(u_s, wbs_ref[...], preferred_element_type=jnp.float32)
    merged = merged + _sigmoid(gate(3) + bg_ref[1:2, :]) * br_s
    y = jnp.dot(merged.astype(jnp.bfloat16), wo_ref[...], preferred_element_type=jnp.float32)
    o_ref[...] = _layer_norm(DN_ALPHA * h + y, lg_ref[...], lb_ref[...]).astype(o_ref.dtype)


def _final_call(x2, yd2, ys2, eg, eb, wg, wbd, wbs, wo, bg, lg, lb):
    R, D = x2.shape
    row = pl.BlockSpec((TM_FINAL, D), lambda i: (i, 0))
    const = lambda i: (0, 0)
    vec = pl.BlockSpec((1, D), const)
    return pl.pallas_call(
        _final_kernel,
        grid=(R // TM_FINAL,),
        in_specs=[row, row, row, vec, vec,
                  pl.BlockSpec((D, 4 * D), const), pl.BlockSpec((D, D), const),
                  pl.BlockSpec((D, D), const), pl.BlockSpec((D, D), const),
                  pl.BlockSpec((2, D), const), vec, vec],
        out_specs=row,
        out_shape=jax.ShapeDtypeStruct((R, D), jnp.float32),
        compiler_params=pltpu.CompilerParams(dimension_semantics=("arbitrary",),
                                             vmem_limit_bytes=VMEM_LIMIT),
        name="final",
    )(x2, yd2, ys2, eg, eb, wg, wbd, wbs, wo, bg, lg, lb)


def _neg_upper_ones(n):
    r = jnp.arange(n)
    return jnp.where(r[None, :] >= r[:, None], -1.0, 0.0).astype(jnp.bfloat16)


def kernel(x, meta_tokens, emb_ln_g, emb_ln_b, w_in, b_gate, diff_lambda, diff_subln_g,
           w_br_diff, w_br_sb, w_out, ln_g, ln_b):
    B, S, D = x.shape
    bf16 = jnp.bfloat16
    w = w_in[0]
    col = lambda i: w[:, i * D:(i + 1) * D]
    qscale = DIFF_HEAD_DIM ** -0.5
    w_qk = jnp.concatenate([col(0) * (qscale * LOG2E), col(1), col(4) * qscale, col(5)], axis=1).astype(bf16)
    w_vt = jnp.concatenate([col(2).T, col(6).T], axis=0).astype(bf16)
    w_kmeta = jnp.concatenate([col(1), col(5)], axis=1).astype(bf16)
    w_gate = jnp.concatenate([col(3), col(7), col(8), col(9)], axis=1).astype(bf16)
    eg, eb = emb_ln_g.reshape(1, D), emb_ln_b.reshape(1, D)

    qd, kd, qs, ks, vdt, vst = _proj_call(x, eg, eb, w_qk, w_vt)
    k_meta, vt_meta = _meta_proj_call(meta_tokens, eg, eb, w_kmeta, w_vt)

    slopes = 2.0 ** (-8.0 * jnp.arange(1, DIFF_HEADS + 1, dtype=jnp.float32) / DIFF_HEADS)

    yd = _diff_call(slopes, diff_lambda[0], qd, kd, vdt, k_meta[:, :D], vt_meta[:D], diff_subln_g[0].reshape(DIFF_V_DIM, 1))
    ys = _sb_call(qs, ks, vst, k_meta[:, D:], vt_meta[D:], _neg_upper_ones(TS), _neg_upper_ones(N_META))

    out = _final_call(x.reshape(B * S, D), yd.reshape(B * S, D), ys.reshape(B * S, D), eg, eb,
                      w_gate, w_br_diff[0].astype(bf16), w_br_sb[0].astype(bf16), w_out[0].astype(bf16),
                      b_gate[0], ln_g[0].reshape(1, D), ln_b[0].reshape(1, D))
    return out.reshape(B, S, D)
```

```python
import functools
import math

import jax
import jax.numpy as jnp
import numpy as np
from jax import lax
from jax.experimental import pallas as pl
from jax.experimental.pallas import tpu as pltpu

D_MODEL = 1024
N_META = 16
DIFF_HEADS = 8
DIFF_HEAD_DIM = 64
DIFF_V_DIM = 128
SB_HEADS = 16
SB_HEAD_DIM = 64
DEPTH = 1
DN_ALPHA = (2.0 * DEPTH) ** 0.25
LN_EPS = 1e-5
RMS_EPS = 1e-5
LAM_INIT = 0.8 - 0.6 * math.exp(-0.3 * 0)

COL_QD, COL_KD, COL_VD, COL_GD, COL_QS, COL_KS, COL_VS, COL_GS, COL_MD, COL_MS = range(10)
N_COLS = 10

LANES = 128
TD = 512
ONES_ROWS = 16
TS = 256
SB_HP = 4
TM_PROJ = 512
TM_FINAL = 256
VMEM_LIMIT = 56 * 1024 * 1024
LOG2E = 1.4426950408889634
NEG_BIG = -1e30
SB_EXIT_LOG = -105.0
EXP2_ZERO = 150.0
NORM_SLACK = 1.01

_NT = (((1,), (1,)), ((), ()))


def _layer_norm(x, g, b):
    mu = jnp.mean(x, axis=-1, keepdims=True)
    xc = x - mu
    var = jnp.mean(xc * xc, axis=-1, keepdims=True)
    return xc * lax.rsqrt(var + LN_EPS) * g + b


def _proj_kernel(x_ref, g_ref, b_ref, wqd_ref, wkd_ref, wqs_ref, wks_ref, wvt_ref,
                 qd_ref, kd_ref, qs_ref, ks_ref, vdt_ref, vst_ref):
    h = _layer_norm(x_ref[...], g_ref[...], b_ref[...]).astype(jnp.bfloat16)
    for w_ref, o_ref in ((wqd_ref, qd_ref), (wkd_ref, kd_ref), (wqs_ref, qs_ref), (wks_ref, ks_ref)):
        o_ref[...] = jnp.dot(h, w_ref[...], preferred_element_type=jnp.float32).astype(jnp.bfloat16)
    for c, (o_ref, tk) in enumerate(((vdt_ref, TD), (vst_ref, TS))):
        vt = lax.dot_general(wvt_ref[c * D_MODEL:(c + 1) * D_MODEL, :], h, _NT,
                             preferred_element_type=jnp.float32).astype(jnp.bfloat16)
        for t in range(TM_PROJ // tk):
            o_ref[t] = vt[:, t * tk:(t + 1) * tk]


def _w_block(c, grid_rank):
    index_map = (lambda i: (0, c)) if grid_rank == 1 else (lambda bi, i: (0, c))
    return pl.BlockSpec((D_MODEL, D_MODEL), index_map, pipeline_mode=pl.Buffered(1))


def _proj_call(x, g, b, w_all, w_vt):
    B, S, D = x.shape
    nt = S // TM_PROJ
    row_spec = pl.BlockSpec((None, TM_PROJ, D), lambda bi, i: (bi, i, 0))
    vt_specs = [pl.BlockSpec((None, TM_PROJ // tk, D, tk), lambda bi, i: (bi, i, 0, 0)) for tk in (TD, TS)]
    const = lambda bi, i: (0, 0)
    return pl.pallas_call(
        _proj_kernel,
        grid=(B, nt),
        in_specs=[row_spec,
                  pl.BlockSpec((1, D), const), pl.BlockSpec((1, D), const)]
        + [_w_block(c, 2) for c in (COL_QD, COL_KD, COL_QS, COL_KS)]
        + [pl.BlockSpec((2 * D, D), const, pipeline_mode=pl.Buffered(1))],
        out_specs=[row_spec, row_spec, row_spec, row_spec] + vt_specs,
        out_shape=[jax.ShapeDtypeStruct((B, S, D), jnp.bfloat16)] * 4
        + [jax.ShapeDtypeStruct((B, S // tk, D, tk), jnp.bfloat16) for tk in (TD, TS)],
        compiler_params=pltpu.CompilerParams(dimension_semantics=("arbitrary", "arbitrary"),
                                             vmem_limit_bytes=VMEM_LIMIT),
        name="proj",
    )(x, g, b, w_all, w_all, w_all, w_all, w_vt)


def _meta_proj_kernel(m_ref, g_ref, b_ref, wkd_ref, wks_ref, wvt_ref, kd_ref, ks_ref, vdt_ref, vst_ref):
    h = _layer_norm(m_ref[...], g_ref[...], b_ref[...]).astype(jnp.bfloat16)
    for w_ref, o_ref in ((wkd_ref, kd_ref), (wks_ref, ks_ref)):
        o_ref[...] = jnp.dot(h, w_ref[...], preferred_element_type=jnp.float32).astype(jnp.bfloat16)
    for c, o_ref in enumerate((vdt_ref, vst_ref)):
        o_ref[...] = lax.dot_general(wvt_ref[c * D_MODEL:(c + 1) * D_MODEL, :], h, _NT,
                                     preferred_element_type=jnp.float32).astype(jnp.bfloat16)


def _meta_proj_call(meta, g, b, w_all, w_vt):
    D = D_MODEL
    const = lambda i: (0, 0)
    full = lambda shape: pl.BlockSpec(shape, const)
    return pl.pallas_call(
        _meta_proj_kernel,
        grid=(1,),
        in_specs=[full((N_META, D)), full((1, D)), full((1, D)), _w_block(COL_KD, 1), _w_block(COL_KS, 1),
                  full((2 * D, D))],
        out_specs=[full((N_META, D)), full((N_META, D)), full((D, N_META)), full((D, N_META))],
        out_shape=[jax.ShapeDtypeStruct((N_META, D), jnp.bfloat16)] * 2
        + [jax.ShapeDtypeStruct((D, N_META), jnp.bfloat16)] * 2,
        compiler_params=pltpu.CompilerParams(dimension_semantics=("arbitrary",), vmem_limit_bytes=VMEM_LIMIT),
        name="meta_proj",
    )(meta, g, b, w_all, w_all, w_vt)


def _diff_kernel(slope_ref, lam_ref, q_ref, k_ref, vt_ref, km_ref, vmt_ref, g_ref, o_ref,
                 s_scr, p_scr, acc_scr, m_scr, bias_scr, qn_scr, kpre_ref):
    S = q_ref.shape[0]
    nq = S // TD
    slope2 = slope_ref[pl.program_id(1)] * LOG2E
    f32, bf16 = jnp.float32, jnp.bfloat16

    row = lax.broadcasted_iota(jnp.int32, (TD, TD), 0)
    col = lax.broadcasted_iota(jnp.int32, (TD, TD), 1)
    bias = slope2 * row.astype(f32)
    bias_scr[0] = bias
    bias_scr[1] = jnp.where(row <= col, bias, NEG_BIG)

    def with_ones(vt):
        return jnp.concatenate([vt, jnp.ones((ONES_ROWS, vt.shape[1]), vt.dtype)], axis=0)

    def split(qt):
        zero = jnp.zeros_like(qt)
        ln = lax.broadcasted_iota(jnp.int32, qt.shape, 1)
        return jnp.where(ln < DIFF_HEAD_DIM, qt, zero), jnp.where(ln >= DIFF_HEAD_DIM, qt, zero)

    colq = lax.broadcasted_iota(jnp.int32, (N_META, S), 1)
    rowm = lax.broadcasted_iota(jnp.int32, (N_META, S), 0)
    bias_m = slope2 * (rowm - N_META - (colq - (colq & (TD - 1)))).astype(f32)
    for c, qc in enumerate(split(q_ref[...])):
        s = lax.dot_general(km_ref[...], qc, _NT, preferred_element_type=f32) + bias_m
        m0 = jnp.max(s, axis=0, keepdims=True)
        p = jnp.exp2(s - m0)
        a0 = jnp.dot(with_ones(vmt_ref[...]), p.astype(bf16), preferred_element_type=f32)
        for t in range(nq):
            m_scr[t, c] = m0[:, t * TD:(t + 1) * TD]
            acc_scr[t, c] = a0[:, t * TD:(t + 1) * TD]

    def norms(x):
        sq = x * x
        ln = lax.broadcasted_iota(jnp.int32, (8, LANES), 1)
        halves = (ln < DIFF_HEAD_DIM, ln >= DIFF_HEAD_DIM)
        return [jnp.sqrt(lax.dot_general(jnp.where(h, 1.0, 0.0).astype(bf16), sq, _NT,
                                         preferred_element_type=f32)[0:1, :]) for h in halves]

    for c, (qn, kn) in enumerate(zip(norms(q_ref[...]), norms(k_ref[...]))):
        kmax = jnp.float32(0.0)
        for t in range(nq):
            qn_scr[t, c] = qn[:, t * TD:(t + 1) * TD]
            kmax = jnp.maximum(kmax, jnp.max(kn[:, t * TD:(t + 1) * TD]))
            kpre_ref[c * nq + t] = kmax

    def stage_a(qt, kt):
        k_tile = k_ref[pl.ds(pl.multiple_of(kt * TD, TD), TD), :]
        qcs = split(q_ref[pl.ds(pl.multiple_of(qt * TD, TD), TD), :])
        tile_bias = bias_scr[(kt == qt).astype(jnp.int32)]
        mx = []
        for c in range(2):
            s = lax.dot_general(k_tile, qcs[c], _NT, preferred_element_type=f32) + tile_bias
            s_scr[c] = s
            mx.append(jnp.max(s, axis=0, keepdims=True))
        return tuple(mx)

    def stage_b(qt, kt, mx):
        off2 = -(slope2 * TD) * (qt - kt).astype(f32)
        alphas, m_news = [], []
        for c in range(2):
            m_old = m_scr[qt, c]
            m_new = jnp.maximum(m_old, mx[c] + off2)
            alpha = jnp.exp2(m_old - m_new)
            p = jnp.exp2(s_scr[c] - (m_new - off2))
            m_scr[qt, c] = m_new
            p_scr[c] = p.astype(bf16)
            alphas.append(alpha)
            m_news.append(m_new)
        return tuple(alphas), tuple(m_news)

    def rest_is_zero(qt, kt, m_news):
        worst_bias = slope2 * (TD - 1) - (slope2 * TD) * (qt - kt).astype(f32)
        slack = None
        for c in range(2):
            bound = qn_scr[qt, c] * (kpre_ref[c * nq + kt] * NORM_SLACK) + (worst_bias + 1.0)
            gap = (m_news[c] - EXP2_ZERO) - bound
            slack = gap if slack is None else jnp.minimum(slack, gap)
        return jnp.min(slack) > 0.0

    def stage_c(qt, kt, alphas):
        vt_tile = with_ones(vt_ref[kt])
        for c in range(2):
            acc_scr[qt, c] = alphas[c] * acc_scr[qt, c] + jnp.dot(vt_tile, p_scr[c], preferred_element_type=f32)

    lam = (jnp.exp(jnp.sum(lam_ref[0:1, :] * lam_ref[1:2, :], axis=1, keepdims=True))
           - jnp.exp(jnp.sum(lam_ref[2:3, :] * lam_ref[3:4, :], axis=1, keepdims=True)) + LAM_INIT)

    def finalize(qt):
        num = [acc_scr[qt, c, 0:DIFF_V_DIM, :] for c in range(2)]
        den = [acc_scr[qt, c, DIFF_V_DIM:DIFF_V_DIM + 1, :] for c in range(2)]
        y = num[0] * (1.0 / den[0]) - lam * (num[1] * (1.0 / den[1]))
        ms = jnp.mean(y * y, axis=0, keepdims=True)
        y = y * lax.rsqrt(ms + RMS_EPS) * g_ref[...] * (1.0 - LAM_INIT)
        o_ref[pl.ds(pl.multiple_of(qt * TD, TD), TD), :] = y.T.astype(o_ref.dtype)

    p_scr[...] = jnp.zeros(p_scr.shape, bf16)
    ones = jnp.ones((1, TD), f32)
    zero_i = jnp.int32(0)
    mx0 = stage_a(zero_i, zero_i)

    def cond(st):
        return st[2] < nq

    def body(st):
        qa, ka, qb, kb, eb, qc, kc, ec, mx, alphas = st
        stage_c(qc, kc, alphas)
        alphas, m_news = stage_b(qb, kb, mx)
        qa_c, ka_c = jnp.minimum(qa, nq - 1), jnp.minimum(ka, nq - 1)
        ea = jnp.logical_or(ka == 0, jnp.logical_and(qa == qb, rest_is_zero(qa_c, jnp.maximum(ka_c - 1, 0), m_news)))
        mx = stage_a(qa_c, ka_c)

        @pl.when(ec)
        def _():
            finalize(qc)

        return (jnp.where(ea, qa + 1, qa), jnp.where(ea, qa + 1, ka - 1), qa, ka, ea, qb, kb, eb, mx, alphas)

    st = (jnp.int32(1), jnp.int32(1), zero_i, zero_i, jnp.bool_(True), zero_i, zero_i, jnp.bool_(False),
          mx0, (ones, ones))
    st = lax.while_loop(cond, body, st)
    _, _, _, _, _, qc, kc, _, _, alphas = st
    stage_c(qc, kc, alphas)
    finalize(qc)


def _diff_call(slopes, lam, qd, kd, vdt, kdm, vdmt, g_col):
    B, S, D = qd.shape
    nq = S // TD
    smem = pl.BlockSpec(memory_space=pltpu.SMEM)
    slab = pl.BlockSpec((None, S, LANES), lambda b, h: (b, 0, h))
    return pl.pallas_call(
        _diff_kernel,
        grid=(B, DIFF_HEADS),
        in_specs=[smem, pl.BlockSpec((4, DIFF_HEAD_DIM), lambda b, h: (0, 0)),
                  slab, slab,
                  pl.BlockSpec((None, nq, DIFF_V_DIM, TD), lambda b, h: (b, 0, h, 0)),
                  pl.BlockSpec((N_META, LANES), lambda b, h: (0, h)),
                  pl.BlockSpec((DIFF_V_DIM, N_META), lambda b, h: (h, 0)),
                  pl.BlockSpec((DIFF_V_DIM, 1), lambda b, h: (0, 0))],
        out_specs=slab,
        out_shape=jax.ShapeDtypeStruct((B, S, D), jnp.bfloat16),
        scratch_shapes=[pltpu.VMEM((2, TD, TD), jnp.float32),
                        pltpu.VMEM((2, TD, TD), jnp.bfloat16),
                        pltpu.VMEM((nq, 2, DIFF_V_DIM + ONES_ROWS, TD), jnp.float32),
                        pltpu.VMEM((nq, 2, 1, TD), jnp.float32),
                        pltpu.VMEM((2, TD, TD), jnp.float32),
                        pltpu.VMEM((nq, 2, 1, TD), jnp.float32),
                        pltpu.SMEM((2 * nq,), jnp.float32)],
        compiler_params=pltpu.CompilerParams(dimension_semantics=("arbitrary",) * 2,
                                             vmem_limit_bytes=VMEM_LIMIT),
        name="diff_attn",
    )(slopes, lam, qd, kd, vdt, kdm, vdmt, g_col)


def _sb_kernel(q_ref, k_ref, vt_ref, km_ref, vmt_ref, u_ref, um_ref, o_ref, zi_scr, out_scr):
    S = q_ref.shape[0]
    nq = S // TS
    f32, bf16 = jnp.float32, jnp.bfloat16
    u = u_ref[...]
    row = lax.broadcasted_iota(jnp.int32, (TS, TS), 0)
    col = lax.broadcasted_iota(jnp.int32, (TS, TS), 1)
    strict = row < col

    def rows(i):
        return pl.ds(pl.multiple_of(i * TS, TS), TS)

    def head_queries(i):
        q = q_ref[rows(i), :]
        lane = lax.broadcasted_iota(jnp.int32, q.shape, 1)
        zero = jnp.zeros_like(q)
        return [jnp.where(jnp.logical_and(lane >= hh * SB_HEAD_DIM, lane < (hh + 1) * SB_HEAD_DIM), q, zero)
                for hh in range(SB_HP)]

    def head_rows(hh):
        return slice(hh * SB_HEAD_DIM, (hh + 1) * SB_HEAD_DIM)

    def suffix_logits(k_tile, qh, mask, umat):
        z = lax.dot_general(k_tile, qh, _NT, preferred_element_type=f32)
        e = jnp.exp2(jnp.abs(z) * (-LOG2E))
        sp = jnp.maximum(z, 0.0) + jnp.log(1.0 + e)
        if mask is not None:
            sp = jnp.where(mask, sp, 0.0)
        incl = jnp.dot(umat, sp.astype(bf16), preferred_element_type=f32)
        return z + incl, incl[0:1, :]

    def stage_a(i):
        qhs = head_queries(i)
        il = jnp.maximum(i - 1, 0)
        k_d, k_l = k_ref[rows(i), :], k_ref[rows(il), :]
        tots = []
        for ch in range(2 * SB_HP):
            diag = ch < SB_HP
            zi, tot = suffix_logits(k_d if diag else k_l, qhs[ch % SB_HP], strict if diag else None, u)
            zi_scr[ch] = zi
            tots.append(tot)
        return tuple(tots)

    def stage_b(i, tots):
        has_left = i > 0
        il = jnp.maximum(i - 1, 0)
        carries = []
        for hh in range(SB_HP):
            w_d = jnp.where(strict, jnp.exp(zi_scr[hh]), 0.0).astype(bf16)
            w_l = jnp.exp(zi_scr[SB_HP + hh] + jnp.where(has_left, tots[hh], NEG_BIG)).astype(bf16)
            out_scr[head_rows(hh), :] = (jnp.dot(vt_ref[i, head_rows(hh), :], w_d, preferred_element_type=f32)
                                         + jnp.dot(vt_ref[il, head_rows(hh), :], w_l, preferred_element_type=f32))
            carries.append(tots[hh] + jnp.where(has_left, tots[SB_HP + hh], 0.0))
        return carries

    def further_tiles(i, carries):
        qhs = head_queries(i)
        for hh in range(SB_HP):

            def tile(k_tile, v_tile, umat, carry, acc):
                zi, tot = suffix_logits(k_tile, qhs[hh], None, umat)
                w = jnp.exp(zi + carry).astype(bf16)
                return carry + tot, acc + jnp.dot(v_tile, w, preferred_element_type=f32)

            def cond(st):
                j, cmax, _, _ = st
                return jnp.logical_and(j >= 0, cmax > SB_EXIT_LOG)

            def body(st):
                j, _, carry, acc = st
                carry, acc = tile(k_ref[rows(j), :], vt_ref[j, head_rows(hh), :], u, carry, acc)
                return j - 1, jnp.max(carry), carry, acc

            st = (i - 2, jnp.max(carries[hh]), carries[hh], out_scr[head_rows(hh), :])
            _, cmax, carry, acc = lax.while_loop(cond, body, st)

            def meta(acc):
                return tile(km_ref[...], vmt_ref[head_rows(hh), :], um_ref[...], carry, acc)[1]

            out_scr[head_rows(hh), :] = lax.cond(cmax > SB_EXIT_LOG, meta, lambda a: a, acc)

    def write_out(i):
        o_ref[rows(i), :] = out_scr[...].T.astype(o_ref.dtype)

    zi_scr[...] = jnp.zeros(zi_scr.shape, f32)
    out_scr[...] = jnp.zeros(out_scr.shape, f32)
    zero_row = jnp.zeros((1, TS), f32)

    def body(i, tots):
        ib = jnp.maximum(i - 1, 0)
        write_out(jnp.maximum(i - 2, 0))
        carries = stage_b(ib, tots)
        tots = stage_a(jnp.minimum(i, nq - 1))
        worst = functools.reduce(jnp.maximum, carries)

        @pl.when(jnp.logical_and(i >= 1, jnp.max(worst) > SB_EXIT_LOG))
        def _():
            further_tiles(ib, carries)

        return tots

    lax.fori_loop(0, nq + 1, body, (zero_row,) * (2 * SB_HP))
    write_out(nq - 1)


def _sb_call(qs, ks, vst, ksm, vsmt, u, um):
    B, S, D = qs.shape
    width = SB_HP * SB_HEAD_DIM
    const = lambda b, h: (0, 0)
    slab = pl.BlockSpec((None, S, width), lambda b, h: (b, 0, h))
    return pl.pallas_call(
        _sb_kernel,
        grid=(B, SB_HEADS // SB_HP),
        in_specs=[slab, slab,
                  pl.BlockSpec((None, S // TS, width, TS), lambda b, h: (b, 0, h, 0)),
                  pl.BlockSpec((N_META, width), lambda b, h: (0, h)),
                  pl.BlockSpec((width, N_META), lambda b, h: (h, 0)),
                  pl.BlockSpec((TS, TS), const), pl.BlockSpec((N_META, N_META), const)],
        out_specs=slab,
        out_shape=jax.ShapeDtypeStruct((B, S, D), jnp.bfloat16),
        scratch_shapes=[pltpu.VMEM((2 * SB_HP, TS, TS), jnp.float32),
                        pltpu.VMEM((width, TS), jnp.float32)],
        compiler_params=pltpu.CompilerParams(dimension_semantics=("arbitrary",) * 2,
                                             vmem_limit_bytes=VMEM_LIMIT),
        name="sb_attn",
    )(qs, ks, vst, ksm, vsmt, u, um)


def _sigmoid(x):
    return 1.0 / (1.0 + jnp.exp(-x))


def _final_kernel(x_ref, yd_ref, ys_ref, eg_ref, eb_ref, wgd_ref, wgs_ref, wmd_ref, wms_ref,
                  wbd_ref, wbs_ref, wo_ref, bg_ref, lg_ref, lb_ref, o_ref):
    h = _layer_norm(x_ref[...], eg_ref[...], eb_ref[...])
    hb = h.astype(jnp.bfloat16)

    def proj(w_ref):
        return jnp.dot(hb, w_ref[...], preferred_element_type=jnp.float32)

    g_d = proj(wgd_ref)
    u_d = (yd_ref[...].astype(jnp.float32) * (g_d * _sigmoid(g_d))).astype(jnp.bfloat16)
    br_d = jnp.dot(u_d, wbd_ref[...], preferred_element_type=jnp.float32)
    merged = _sigmoid(proj(wmd_ref) + bg_ref[0:1, :]) * br_d
    g_s = proj(wgs_ref)
    u_s = (ys_ref[...].astype(jnp.float32) * (g_s * _sigmoid(g_s))).astype(jnp.bfloat16)
    br_s = jnp.dot(u_s, wbs_ref[...], preferred_element_type=jnp.float32)
    merged = merged + _sigmoid(proj(wms_ref) + bg_ref[1:2, :]) * br_s
    y = jnp.dot(merged.astype(jnp.bfloat16), wo_ref[...], preferred_element_type=jnp.float32)
    o_ref[...] = _layer_norm(DN_ALPHA * h + y, lg_ref[...], lb_ref[...]).astype(o_ref.dtype)


def _final_call(x2, yd2, ys2, eg, eb, w_all, wbd, wbs, wo, bg, lg, lb):
    R, D = x2.shape
    row = pl.BlockSpec((TM_FINAL, D), lambda i: (i, 0))
    const = lambda i: (0, 0)
    vec = pl.BlockSpec((1, D), const)
    weight = pl.BlockSpec((D, D), const, pipeline_mode=pl.Buffered(1))
    return pl.pallas_call(
        _final_kernel,
        grid=(R // TM_FINAL,),
        in_specs=[row, row, row, vec, vec]
        + [_w_block(c, 1) for c in (COL_GD, COL_GS, COL_MD, COL_MS)]
        + [weight, weight, weight, pl.BlockSpec((2, D), const), vec, vec],
        out_specs=row,
        out_shape=jax.ShapeDtypeStruct((R, D), jnp.float32),
        compiler_params=pltpu.CompilerParams(dimension_semantics=("arbitrary",),
                                             vmem_limit_bytes=VMEM_LIMIT),
        name="final",
    )(x2, yd2, ys2, eg, eb, w_all, w_all, w_all, w_all, wbd, wbs, wo, bg, lg, lb)


def _neg_upper_ones(n):
    r = jnp.arange(n)
    return jnp.where(r[None, :] >= r[:, None], -1.0, 0.0).astype(jnp.bfloat16)


def kernel(x, meta_tokens, emb_ln_g, emb_ln_b, w_in, b_gate, diff_lambda, diff_subln_g,
           w_br_diff, w_br_sb, w_out, ln_g, ln_b):
    B, S, D = x.shape
    bf16 = jnp.bfloat16
    qscale = DIFF_HEAD_DIM ** -0.5
    col_scale = np.ones((N_COLS * D,), np.float32)
    col_scale[COL_QD * D:(COL_QD + 1) * D] = qscale * LOG2E
    col_scale[COL_QS * D:(COL_QS + 1) * D] = qscale
    w_all = (w_in[0] * col_scale).astype(bf16)
    w_vt = jnp.concatenate([w_all[:, COL_VD * D:(COL_VD + 1) * D].T,
                            w_all[:, COL_VS * D:(COL_VS + 1) * D].T], axis=0)
    eg, eb = emb_ln_g.reshape(1, D), emb_ln_b.reshape(1, D)

    qd, kd, qs, ks, vdt, vst = _proj_call(x, eg, eb, w_all, w_vt)
    kd_meta, ks_meta, vdt_meta, vst_meta = _meta_proj_call(meta_tokens, eg, eb, w_all, w_vt)

    slopes = 2.0 ** (-8.0 * jnp.arange(1, DIFF_HEADS + 1, dtype=jnp.float32) / DIFF_HEADS)

    yd = _diff_call(slopes, diff_lambda[0], qd, kd, vdt, kd_meta, vdt_meta, diff_subln_g[0].reshape(DIFF_V_DIM, 1))
    ys = _sb_call(qs, ks, vst, ks_meta, vst_meta, _neg_upper_ones(TS), _neg_upper_ones(N_META))

    out = _final_call(x.reshape(B * S, D), yd.reshape(B * S, D), ys.reshape(B * S, D), eg, eb,
                      w_all, w_br_diff[0].astype(bf16), w_br_sb[0].astype(bf16), w_out[0].astype(bf16),
                      b_gate[0], ln_g[0].reshape(1, D), ln_b[0].reshape(1, D))
    return out.reshape(B, S, D)
```

```python
import functools
import math

import jax
import jax.numpy as jnp
import numpy as np
from jax import lax
from jax.experimental import pallas as pl
from jax.experimental.pallas import tpu as pltpu

D_MODEL = 1024
N_META = 16
DIFF_HEADS = 8
DIFF_HEAD_DIM = 64
DIFF_V_DIM = 128
SB_HEADS = 16
SB_HEAD_DIM = 64
DEPTH = 1
DN_ALPHA = (2.0 * DEPTH) ** 0.25
LN_EPS = 1e-5
RMS_EPS = 1e-5
LAM_INIT = 0.8 - 0.6 * math.exp(-0.3 * 0)

COL_QD, COL_KD, COL_VD, COL_GD, COL_QS, COL_KS, COL_VS, COL_GS, COL_MD, COL_MS = range(10)
N_COLS = 10

LANES = 128
TD = 512
ONES_ROWS = 16
TS = 256
SB_HP = 4
TM_PROJ = 512
TM_FINAL = 256
VMEM_LIMIT = 56 * 1024 * 1024
LOG2E = 1.4426950408889634
NEG_BIG = -1e30
SB_EXIT_LOG = -105.0
EXP2_ZERO = 150.0
NORM_SLACK = 1.01

_NT = (((1,), (1,)), ((), ()))


def _layer_norm(x, g, b):
    mu = jnp.mean(x, axis=-1, keepdims=True)
    xc = x - mu
    var = jnp.mean(xc * xc, axis=-1, keepdims=True)
    return xc * lax.rsqrt(var + LN_EPS) * g + b


def _proj_kernel(x_ref, g_ref, b_ref, wkd_ref, wks_ref, wt_ref,
                 kd_ref, ks_ref, qdt_ref, qst_ref, vdt_ref, vst_ref):
    h = _layer_norm(x_ref[...], g_ref[...], b_ref[...]).astype(jnp.bfloat16)
    for w_ref, o_ref in ((wkd_ref, kd_ref), (wks_ref, ks_ref)):
        o_ref[...] = jnp.dot(h, w_ref[...], preferred_element_type=jnp.float32).astype(jnp.bfloat16)
    for c, (o_ref, tk) in enumerate(((qdt_ref, TD), (qst_ref, TS), (vdt_ref, TD), (vst_ref, TS))):
        xt = lax.dot_general(wt_ref[c * D_MODEL:(c + 1) * D_MODEL, :], h, _NT,
                             preferred_element_type=jnp.float32).astype(jnp.bfloat16)
        for t in range(TM_PROJ // tk):
            o_ref[t] = xt[:, t * tk:(t + 1) * tk]


def _w_block(c, grid_rank):
    index_map = (lambda i: (0, c)) if grid_rank == 1 else (lambda bi, i: (0, c))
    return pl.BlockSpec((D_MODEL, D_MODEL), index_map, pipeline_mode=pl.Buffered(1))


def _proj_call(x, g, b, w_all, w_t):
    B, S, D = x.shape
    nt = S // TM_PROJ
    row_spec = pl.BlockSpec((None, TM_PROJ, D), lambda bi, i: (bi, i, 0))
    tiles = (TD, TS, TD, TS)
    t_specs = [pl.BlockSpec((None, TM_PROJ // tk, D, tk), lambda bi, i: (bi, i, 0, 0)) for tk in tiles]
    const = lambda bi, i: (0, 0)
    return pl.pallas_call(
        _proj_kernel,
        grid=(B, nt),
        in_specs=[row_spec,
                  pl.BlockSpec((1, D), const), pl.BlockSpec((1, D), const),
                  _w_block(COL_KD, 2), _w_block(COL_KS, 2),
                  pl.BlockSpec((4 * D, D), const, pipeline_mode=pl.Buffered(1))],
        out_specs=[row_spec, row_spec] + t_specs,
        out_shape=[jax.ShapeDtypeStruct((B, S, D), jnp.bfloat16)] * 2
        + [jax.ShapeDtypeStruct((B, S // tk, D, tk), jnp.bfloat16) for tk in tiles],
        compiler_params=pltpu.CompilerParams(dimension_semantics=("arbitrary", "arbitrary"),
                                             vmem_limit_bytes=VMEM_LIMIT),
        name="proj",
    )(x, g, b, w_all, w_all, w_t)


def _meta_proj_kernel(m_ref, g_ref, b_ref, wkd_ref, wks_ref, wvt_ref, kd_ref, ks_ref, vdt_ref, vst_ref):
    h = _layer_norm(m_ref[...], g_ref[...], b_ref[...]).astype(jnp.bfloat16)
    for w_ref, o_ref in ((wkd_ref, kd_ref), (wks_ref, ks_ref)):
        o_ref[...] = jnp.dot(h, w_ref[...], preferred_element_type=jnp.float32).astype(jnp.bfloat16)
    for c, o_ref in enumerate((vdt_ref, vst_ref)):
        o_ref[...] = lax.dot_general(wvt_ref[c * D_MODEL:(c + 1) * D_MODEL, :], h, _NT,
                                     preferred_element_type=jnp.float32).astype(jnp.bfloat16)


def _meta_proj_call(meta, g, b, w_all, w_t):
    D = D_MODEL
    const = lambda i: (0, 0)
    full = lambda shape: pl.BlockSpec(shape, const)
    return pl.pallas_call(
        _meta_proj_kernel,
        grid=(1,),
        in_specs=[full((N_META, D)), full((1, D)), full((1, D)), _w_block(COL_KD, 1), _w_block(COL_KS, 1),
                  pl.BlockSpec((2 * D, D), lambda i: (1, 0))],
        out_specs=[full((N_META, D)), full((N_META, D)), full((D, N_META)), full((D, N_META))],
        out_shape=[jax.ShapeDtypeStruct((N_META, D), jnp.bfloat16)] * 2
        + [jax.ShapeDtypeStruct((D, N_META), jnp.bfloat16)] * 2,
        compiler_params=pltpu.CompilerParams(dimension_semantics=("arbitrary",), vmem_limit_bytes=VMEM_LIMIT),
        name="meta_proj",
    )(meta, g, b, w_all, w_all, w_t)


def _diff_kernel(slope_ref, lam_ref, qt_ref, k_ref, vt_ref, km_ref, vmt_ref, g_ref, o_ref,
                 s_scr, p_scr, acc_scr, m_scr, bias_scr, qn_scr, kpre_ref):
    nq = qt_ref.shape[0]
    slope2 = slope_ref[pl.program_id(1)] * LOG2E
    f32, bf16 = jnp.float32, jnp.bfloat16

    row = lax.broadcasted_iota(jnp.int32, (TD, TD), 0)
    col = lax.broadcasted_iota(jnp.int32, (TD, TD), 1)
    bias = slope2 * row.astype(f32)
    bias_scr[0] = bias
    bias_scr[1] = jnp.where(row <= col, bias, NEG_BIG)

    def with_ones(vt):
        return jnp.concatenate([vt, jnp.ones((ONES_ROWS, vt.shape[1]), vt.dtype)], axis=0)

    def split(qt):
        zero = jnp.zeros_like(qt)
        rw = lax.broadcasted_iota(jnp.int32, qt.shape, 0)
        return jnp.where(rw < DIFF_HEAD_DIM, qt, zero), jnp.where(rw >= DIFF_HEAD_DIM, qt, zero)

    S = nq * TD
    qt_all = jnp.concatenate([qt_ref[t] for t in range(nq)], axis=1)
    colq = lax.broadcasted_iota(jnp.int32, (N_META, S), 1)
    rowm = lax.broadcasted_iota(jnp.int32, (N_META, S), 0)
    bias_m = slope2 * (rowm - N_META - (colq - (colq & (TD - 1)))).astype(f32)
    vmt_ones = with_ones(vmt_ref[...])
    qsq = jnp.square(qt_all.astype(f32))
    for c, qc in enumerate(split(qt_all)):
        s = jnp.dot(km_ref[...], qc, preferred_element_type=f32) + bias_m
        m0 = jnp.max(s, axis=0, keepdims=True)
        p = jnp.exp2(s - m0)
        a0 = jnp.dot(vmt_ones, p.astype(bf16), preferred_element_type=f32)
        qn = jnp.sqrt(jnp.sum(qsq[c * DIFF_HEAD_DIM:(c + 1) * DIFF_HEAD_DIM], axis=0, keepdims=True))
        for t in range(nq):
            m_scr[t, c] = m0[:, t * TD:(t + 1) * TD]
            acc_scr[t, c] = a0[:, t * TD:(t + 1) * TD]
            qn_scr[t, c] = qn[:, t * TD:(t + 1) * TD]

    k_all = k_ref[...]
    ksq = k_all * k_all
    ln8 = lax.broadcasted_iota(jnp.int32, (8, LANES), 1)
    for c, half in enumerate((ln8 < DIFF_HEAD_DIM, ln8 >= DIFF_HEAD_DIM)):
        kn = jnp.sqrt(lax.dot_general(jnp.where(half, 1.0, 0.0).astype(bf16), ksq, _NT,
                                      preferred_element_type=f32)[0:1, :])
        kmax = jnp.float32(0.0)
        for t in range(nq):
            kmax = jnp.maximum(kmax, jnp.max(kn[:, t * TD:(t + 1) * TD]))
            kpre_ref[c * nq + t] = kmax

    def stage_a(qt, kt):
        k_tile = k_ref[pl.ds(pl.multiple_of(kt * TD, TD), TD), :]
        qcs = split(qt_ref[qt])
        tile_bias = bias_scr[(kt == qt).astype(jnp.int32)]
        mx = []
        for c in range(2):
            s = jnp.dot(k_tile, qcs[c], preferred_element_type=f32) + tile_bias
            s_scr[c] = s
            mx.append(jnp.max(s, axis=0, keepdims=True))
        return tuple(mx)

    def stage_b(qt, kt, mx):
        off2 = -(slope2 * TD) * (qt - kt).astype(f32)
        alphas, m_news = [], []
        for c in range(2):
            m_old = m_scr[qt, c]
            m_new = jnp.maximum(m_old, mx[c] + off2)
            alpha = jnp.exp2(m_old - m_new)
            p = jnp.exp2(s_scr[c] - (m_new - off2))
            m_scr[qt, c] = m_new
            p_scr[c] = p.astype(bf16)
            alphas.append(alpha)
            m_news.append(m_new)
        return tuple(alphas), tuple(m_news)

    def rest_is_zero(qt, kt, m_news):
        worst_bias = slope2 * (TD - 1) - (slope2 * TD) * (qt - kt).astype(f32)
        slack = None
        for c in range(2):
            bound = qn_scr[qt, c] * (kpre_ref[c * nq + kt] * NORM_SLACK) + (worst_bias + 1.0)
            gap = (m_news[c] - EXP2_ZERO) - bound
            slack = gap if slack is None else jnp.minimum(slack, gap)
        return jnp.min(slack) > 0.0

    def stage_c(qt, kt, alphas):
        vt_tile = with_ones(vt_ref[kt])
        for c in range(2):
            acc_scr[qt, c] = alphas[c] * acc_scr[qt, c] + jnp.dot(vt_tile, p_scr[c], preferred_element_type=f32)

    lam = (jnp.exp(jnp.sum(lam_ref[0:1, :] * lam_ref[1:2, :], axis=1, keepdims=True))
           - jnp.exp(jnp.sum(lam_ref[2:3, :] * lam_ref[3:4, :], axis=1, keepdims=True)) + LAM_INIT)

    def finalize(qt):
        num = [acc_scr[qt, c, 0:DIFF_V_DIM, :] for c in range(2)]
        den = [acc_scr[qt, c, DIFF_V_DIM:DIFF_V_DIM + 1, :] for c in range(2)]
        y = num[0] * (1.0 / den[0]) - lam * (num[1] * (1.0 / den[1]))
        ms = jnp.mean(y * y, axis=0, keepdims=True)
        y = y * lax.rsqrt(ms + RMS_EPS) * g_ref[...] * (1.0 - LAM_INIT)
        o_ref[pl.ds(pl.multiple_of(qt * TD, TD), TD), :] = y.T.astype(o_ref.dtype)

    p_scr[...] = jnp.zeros(p_scr.shape, bf16)
    ones = jnp.ones((1, TD), f32)
    zero_i = jnp.int32(0)
    mx0 = stage_a(zero_i, zero_i)

    def cond(st):
        return st[2] < nq

    def body(st):
        qa, ka, qb, kb, eb, qc, kc, ec, mx, alphas = st
        stage_c(qc, kc, alphas)
        alphas, m_news = stage_b(qb, kb, mx)
        qa_c, ka_c = jnp.minimum(qa, nq - 1), jnp.minimum(ka, nq - 1)
        ea = jnp.logical_or(ka == 0, jnp.logical_and(qa == qb, rest_is_zero(qa_c, jnp.maximum(ka_c - 1, 0), m_news)))
        mx = stage_a(qa_c, ka_c)

        @pl.when(ec)
        def _():
            finalize(qc)

        return (jnp.where(ea, qa + 1, qa), jnp.where(ea, qa + 1, ka - 1), qa, ka, ea, qb, kb, eb, mx, alphas)

    st = (jnp.int32(1), jnp.int32(1), zero_i, zero_i, jnp.bool_(True), zero_i, zero_i, jnp.bool_(False),
          mx0, (ones, ones))
    st = lax.while_loop(cond, body, st)
    _, _, _, _, _, qc, kc, _, _, alphas = st
    stage_c(qc, kc, alphas)
    finalize(qc)


def _diff_call(slopes, lam, qdt, kd, vdt, kdm, vdmt, g_col):
    B, S, D = kd.shape
    nq = S // TD
    smem = pl.BlockSpec(memory_space=pltpu.SMEM)
    slab = pl.BlockSpec((None, S, LANES), lambda b, h: (b, 0, h))
    tiles_t = pl.BlockSpec((None, nq, LANES, TD), lambda b, h: (b, 0, h, 0))
    return pl.pallas_call(
        _diff_kernel,
        grid=(B, DIFF_HEADS),
        in_specs=[smem, pl.BlockSpec((4, DIFF_HEAD_DIM), lambda b, h: (0, 0)),
                  tiles_t, slab, tiles_t,
                  pl.BlockSpec((N_META, LANES), lambda b, h: (0, h)),
                  pl.BlockSpec((DIFF_V_DIM, N_META), lambda b, h: (h, 0)),
                  pl.BlockSpec((DIFF_V_DIM, 1), lambda b, h: (0, 0))],
        out_specs=slab,
        out_shape=jax.ShapeDtypeStruct((B, S, D), jnp.bfloat16),
        scratch_shapes=[pltpu.VMEM((2, TD, TD), jnp.float32),
                        pltpu.VMEM((2, TD, TD), jnp.bfloat16),
                        pltpu.VMEM((nq, 2, DIFF_V_DIM + ONES_ROWS, TD), jnp.float32),
                        pltpu.VMEM((nq, 2, 1, TD), jnp.float32),
                        pltpu.VMEM((2, TD, TD), jnp.float32),
                        pltpu.VMEM((nq, 2, 1, TD), jnp.float32),
                        pltpu.SMEM((2 * nq,), jnp.float32)],
        compiler_params=pltpu.CompilerParams(dimension_semantics=("arbitrary",) * 2,
                                             vmem_limit_bytes=VMEM_LIMIT),
        name="diff_attn",
    )(slopes, lam, qdt, kd, vdt, kdm, vdmt, g_col)


def _sb_kernel(qt_ref, k_ref, vt_ref, km_ref, vmt_ref, u_ref, um_ref, o_ref, zi_scr, out_scr):
    nq = qt_ref.shape[0]
    f32, bf16 = jnp.float32, jnp.bfloat16
    u = u_ref[...]
    row = lax.broadcasted_iota(jnp.int32, (TS, TS), 0)
    col = lax.broadcasted_iota(jnp.int32, (TS, TS), 1)
    strict = row < col

    def rows(i):
        return pl.ds(pl.multiple_of(i * TS, TS), TS)

    def head_queries(i):
        qt = qt_ref[i]
        rw = lax.broadcasted_iota(jnp.int32, qt.shape, 0)
        zero = jnp.zeros_like(qt)
        return [jnp.where(jnp.logical_and(rw >= hh * SB_HEAD_DIM, rw < (hh + 1) * SB_HEAD_DIM), qt, zero)
                for hh in range(SB_HP)]

    def head_rows(hh):
        return slice(hh * SB_HEAD_DIM, (hh + 1) * SB_HEAD_DIM)

    def suffix_logits(k_tile, qh, mask, umat):
        z = jnp.dot(k_tile, qh, preferred_element_type=f32)
        e = jnp.exp2(jnp.abs(z) * (-LOG2E))
        sp = jnp.maximum(z, 0.0) + jnp.log(1.0 + e)
        if mask is not None:
            sp = jnp.where(mask, sp, 0.0)
        incl = jnp.dot(umat, sp.astype(bf16), preferred_element_type=f32)
        return z + incl, incl[0:1, :]

    def stage_a(i):
        qhs = head_queries(i)
        il = jnp.maximum(i - 1, 0)
        k_d, k_l = k_ref[rows(i), :], k_ref[rows(il), :]
        tots = []
        for ch in range(2 * SB_HP):
            diag = ch < SB_HP
            zi, tot = suffix_logits(k_d if diag else k_l, qhs[ch % SB_HP], strict if diag else None, u)
            zi_scr[ch] = zi
            tots.append(tot)
        return tuple(tots)

    def stage_b(i, tots):
        has_left = i > 0
        il = jnp.maximum(i - 1, 0)
        carries = []
        for hh in range(SB_HP):
            w_d = jnp.where(strict, jnp.exp(zi_scr[hh]), 0.0).astype(bf16)
            w_l = jnp.exp(zi_scr[SB_HP + hh] + jnp.where(has_left, tots[hh], NEG_BIG)).astype(bf16)
            out_scr[head_rows(hh), :] = (jnp.dot(vt_ref[i, head_rows(hh), :], w_d, preferred_element_type=f32)
                                         + jnp.dot(vt_ref[il, head_rows(hh), :], w_l, preferred_element_type=f32))
            carries.append(tots[hh] + jnp.where(has_left, tots[SB_HP + hh], 0.0))
        return carries

    def further_tiles(i, carries):
        qhs = head_queries(i)
        for hh in range(SB_HP):

            def tile(k_tile, v_tile, umat, carry, acc):
                zi, tot = suffix_logits(k_tile, qhs[hh], None, umat)
                w = jnp.exp(zi + carry).astype(bf16)
                return carry + tot, acc + jnp.dot(v_tile, w, preferred_element_type=f32)

            def cond(st):
                j, cmax, _, _ = st
                return jnp.logical_and(j >= 0, cmax > SB_EXIT_LOG)

            def body(st):
                j, _, carry, acc = st
                carry, acc = tile(k_ref[rows(j), :], vt_ref[j, head_rows(hh), :], u, carry, acc)
                return j - 1, jnp.max(carry), carry, acc

            st = (i - 2, jnp.max(carries[hh]), carries[hh], out_scr[head_rows(hh), :])
            _, cmax, carry, acc = lax.while_loop(cond, body, st)

            def meta(acc):
                return tile(km_ref[...], vmt_ref[head_rows(hh), :], um_ref[...], carry, acc)[1]

            out_scr[head_rows(hh), :] = lax.cond(cmax > SB_EXIT_LOG, meta, lambda a: a, acc)

    def write_out(i):
        o_ref[rows(i), :] = out_scr[...].T.astype(o_ref.dtype)

    zi_scr[...] = jnp.zeros(zi_scr.shape, f32)
    out_scr[...] = jnp.zeros(out_scr.shape, f32)
    zero_row = jnp.zeros((1, TS), f32)

    def body(i, tots):
        ib = jnp.maximum(i - 1, 0)
        write_out(jnp.maximum(i - 2, 0))
        carries = stage_b(ib, tots)
        tots = stage_a(jnp.minimum(i, nq - 1))
        worst = functools.reduce(jnp.maximum, carries)

        @pl.when(jnp.logical_and(i >= 1, jnp.max(worst) > SB_EXIT_LOG))
        def _():
            further_tiles(ib, carries)

        return tots

    lax.fori_loop(0, nq + 1, body, (zero_row,) * (2 * SB_HP))
    write_out(nq - 1)


def _sb_call(qst, ks, vst, ksm, vsmt, u, um):
    B, S, D = ks.shape
    width = SB_HP * SB_HEAD_DIM
    const = lambda b, h: (0, 0)
    slab = pl.BlockSpec((None, S, width), lambda b, h: (b, 0, h))
    tiles_t = pl.BlockSpec((None, S // TS, width, TS), lambda b, h: (b, 0, h, 0))
    return pl.pallas_call(
        _sb_kernel,
        grid=(B, SB_HEADS // SB_HP),
        in_specs=[tiles_t, slab, tiles_t,
                  pl.BlockSpec((N_META, width), lambda b, h: (0, h)),
                  pl.BlockSpec((width, N_META), lambda b, h: (h, 0)),
                  pl.BlockSpec((TS, TS), const), pl.BlockSpec((N_META, N_META), const)],
        out_specs=slab,
        out_shape=jax.ShapeDtypeStruct((B, S, D), jnp.bfloat16),
        scratch_shapes=[pltpu.VMEM((2 * SB_HP, TS, TS), jnp.float32),
                        pltpu.VMEM((width, TS), jnp.float32)],
        compiler_params=pltpu.CompilerParams(dimension_semantics=("arbitrary",) * 2,
                                             vmem_limit_bytes=VMEM_LIMIT),
        name="sb_attn",
    )(qst, ks, vst, ksm, vsmt, u, um)


def _sigmoid(x):
    return 1.0 / (1.0 + jnp.exp(-x))


def _final_kernel(x_ref, yd_ref, ys_ref, eg_ref, eb_ref, wgd_ref, wgs_ref, wmd_ref, wms_ref,
                  wbd_ref, wbs_ref, wo_ref, bg_ref, lg_ref, lb_ref, o_ref):
    h = _layer_norm(x_ref[...], eg_ref[...], eb_ref[...])
    hb = h.astype(jnp.bfloat16)

    def proj(w_ref):
        return jnp.dot(hb, w_ref[...], preferred_element_type=jnp.float32)

    g_d = proj(wgd_ref)
    u_d = (yd_ref[...].astype(jnp.float32) * (g_d * _sigmoid(g_d))).astype(jnp.bfloat16)
    br_d = jnp.dot(u_d, wbd_ref[...], preferred_element_type=jnp.float32)
    merged = _sigmoid(proj(wmd_ref) + bg_ref[0:1, :]) * br_d
    g_s = proj(wgs_ref)
    u_s = (ys_ref[...].astype(jnp.float32) * (g_s * _sigmoid(g_s))).astype(jnp.bfloat16)
    br_s = jnp.dot(u_s, wbs_ref[...], preferred_element_type=jnp.float32)
    merged = merged + _sigmoid(proj(wms_ref) + bg_ref[1:2, :]) * br_s
    y = jnp.dot(merged.astype(jnp.bfloat16), wo_ref[...], preferred_element_type=jnp.float32)
    o_ref[...] = _layer_norm(DN_ALPHA * h + y, lg_ref[...], lb_ref[...]).astype(o_ref.dtype)


def _final_call(x2, yd2, ys2, eg, eb, w_all, wbd, wbs, wo, bg, lg, lb):
    R, D = x2.shape
    row = pl.BlockSpec((TM_FINAL, D), lambda i: (i, 0))
    const = lambda i: (0, 0)
    vec = pl.BlockSpec((1, D), const)
    weight = pl.BlockSpec((D, D), const, pipeline_mode=pl.Buffered(1))
    return pl.pallas_call(
        _final_kernel,
        grid=(R // TM_FINAL,),
        in_specs=[row, row, row, vec, vec]
        + [_w_block(c, 1) for c in (COL_GD, COL_GS, COL_MD, COL_MS)]
        + [weight, weight, weight, pl.BlockSpec((2, D), const), vec, vec],
        out_specs=row,
        out_shape=jax.ShapeDtypeStruct((R, D), jnp.float32),
        compiler_params=pltpu.CompilerParams(dimension_semantics=("arbitrary",),
                                             vmem_limit_bytes=VMEM_LIMIT),
        name="final",
    )(x2, yd2, ys2, eg, eb, w_all, w_all, w_all, w_all, wbd, wbs, wo, bg, lg, lb)


def _neg_upper_ones(n):
    r = jnp.arange(n)
    return jnp.where(r[None, :] >= r[:, None], -1.0, 0.0).astype(jnp.bfloat16)


def kernel(x, meta_tokens, emb_ln_g, emb_ln_b, w_in, b_gate, diff_lambda, diff_subln_g,
           w_br_diff, w_br_sb, w_out, ln_g, ln_b):
    B, S, D = x.shape
    bf16 = jnp.bfloat16
    qscale = DIFF_HEAD_DIM ** -0.5
    col_scale = np.ones((N_COLS * D,), np.float32)
    col_scale[COL_QD * D:(COL_QD + 1) * D] = qscale * LOG2E
    col_scale[COL_QS * D:(COL_QS + 1) * D] = qscale
    w_all = (w_in[0] * col_scale).astype(bf16)
    w_t = jnp.concatenate([w_all[:, c * D:(c + 1) * D].T for c in (COL_QD, COL_QS, COL_VD, COL_VS)], axis=0)
    eg, eb = emb_ln_g.reshape(1, D), emb_ln_b.reshape(1, D)

    kd, ks, qdt, qst, vdt, vst = _proj_call(x, eg, eb, w_all, w_t)
    kd_meta, ks_meta, vdt_meta, vst_meta = _meta_proj_call(meta_tokens, eg, eb, w_all, w_t)

    slopes = 2.0 ** (-8.0 * jnp.arange(1, DIFF_HEADS + 1, dtype=jnp.float32) / DIFF_HEADS)

    yd = _diff_call(slopes, diff_lambda[0], qdt, kd, vdt, kd_meta, vdt_meta, diff_subln_g[0].reshape(DIFF_V_DIM, 1))
    ys = _sb_call(qst, ks, vst, ks_meta, vst_meta, _neg_upper_ones(TS), _neg_upper_ones(N_META))

    out = _final_call(x.reshape(B * S, D), yd.reshape(B * S, D), ys.reshape(B * S, D), eg, eb,
                      w_all, w_br_diff[0].astype(bf16), w_br_sb[0].astype(bf16), w_out[0].astype(bf16),
                      b_gate[0], ln_g[0].reshape(1, D), ln_b[0].reshape(1, D))
    return out.reshape(B, S, D)
```

```python
import functools
import math

import jax
import jax.numpy as jnp
import numpy as np
from jax import lax
from jax.experimental import pallas as pl
from jax.experimental.pallas import tpu as pltpu

D_MODEL = 1024
N_META = 16
DIFF_HEADS = 8
DIFF_HEAD_DIM = 64
DIFF_V_DIM = 128
SB_HEADS = 16
SB_HEAD_DIM = 64
DEPTH = 1
DN_ALPHA = (2.0 * DEPTH) ** 0.25
LN_EPS = 1e-5
RMS_EPS = 1e-5
LAM_INIT = 0.8 - 0.6 * math.exp(-0.3 * 0)

COL_QD, COL_KD, COL_VD, COL_GD, COL_QS, COL_KS, COL_VS, COL_GS, COL_MD, COL_MS = range(10)
N_COLS = 10

LANES = 128
TD = 512
ONES_ROWS = 16
TS = 256
SB_HP = 4
TM_PROJ = 512
TM_FINAL = 512
FINAL_CHUNK = 256
VMEM_LIMIT = 56 * 1024 * 1024
LOG2E = 1.4426950408889634
NEG_BIG = -1e30
SB_EXIT_LOG = -105.0
EXP2_ZERO = 150.0
NORM_SLACK = 1.01

_NT = (((1,), (1,)), ((), ()))


def _layer_norm(x, g, b):
    mu = jnp.mean(x, axis=-1, keepdims=True)
    xc = x - mu
    var = jnp.mean(xc * xc, axis=-1, keepdims=True)
    return xc * lax.rsqrt(var + LN_EPS) * g + b


def _proj_kernel(x_ref, g_ref, b_ref, wkd_ref, wks_ref, wt_ref,
                 kd_ref, ks_ref, qdt_ref, qst_ref, vdt_ref, vst_ref):
    h = _layer_norm(x_ref[...], g_ref[...], b_ref[...]).astype(jnp.bfloat16)
    for w_ref, o_ref in ((wkd_ref, kd_ref), (wks_ref, ks_ref)):
        o_ref[...] = jnp.dot(h, w_ref[...], preferred_element_type=jnp.float32).astype(jnp.bfloat16)
    for c, (o_ref, tk) in enumerate(((qdt_ref, TD), (qst_ref, TS), (vdt_ref, TD), (vst_ref, TS))):
        xt = lax.dot_general(wt_ref[c * D_MODEL:(c + 1) * D_MODEL, :], h, _NT,
                             preferred_element_type=jnp.float32).astype(jnp.bfloat16)
        for t in range(TM_PROJ // tk):
            o_ref[t] = xt[:, t * tk:(t + 1) * tk]


def _w_block(c, grid_rank):
    index_map = (lambda i: (0, c)) if grid_rank == 1 else (lambda bi, i: (0, c))
    return pl.BlockSpec((D_MODEL, D_MODEL), index_map, pipeline_mode=pl.Buffered(1))


def _proj_call(x, g, b, w_all, w_t):
    B, S, D = x.shape
    nt = S // TM_PROJ
    row_spec = pl.BlockSpec((None, TM_PROJ, D), lambda bi, i: (bi, i, 0))
    tiles = (TD, TS, TD, TS)
    t_specs = [pl.BlockSpec((None, TM_PROJ // tk, D, tk), lambda bi, i: (bi, i, 0, 0)) for tk in tiles]
    const = lambda bi, i: (0, 0)
    return pl.pallas_call(
        _proj_kernel,
        grid=(B, nt),
        in_specs=[row_spec,
                  pl.BlockSpec((1, D), const), pl.BlockSpec((1, D), const),
                  _w_block(COL_KD, 2), _w_block(COL_KS, 2),
                  pl.BlockSpec((4 * D, D), const, pipeline_mode=pl.Buffered(1))],
        out_specs=[row_spec, row_spec] + t_specs,
        out_shape=[jax.ShapeDtypeStruct((B, S, D), jnp.bfloat16)] * 2
        + [jax.ShapeDtypeStruct((B, S // tk, D, tk), jnp.bfloat16) for tk in tiles],
        compiler_params=pltpu.CompilerParams(dimension_semantics=("arbitrary", "arbitrary"),
                                             vmem_limit_bytes=VMEM_LIMIT),
        name="proj",
    )(x, g, b, w_all, w_all, w_t)


def _meta_proj_kernel(m_ref, g_ref, b_ref, wkd_ref, wks_ref, wvt_ref, kd_ref, ks_ref, vdt_ref, vst_ref):
    h = _layer_norm(m_ref[...], g_ref[...], b_ref[...]).astype(jnp.bfloat16)
    for w_ref, o_ref in ((wkd_ref, kd_ref), (wks_ref, ks_ref)):
        o_ref[...] = jnp.dot(h, w_ref[...], preferred_element_type=jnp.float32).astype(jnp.bfloat16)
    for c, o_ref in enumerate((vdt_ref, vst_ref)):
        o_ref[...] = lax.dot_general(wvt_ref[c * D_MODEL:(c + 1) * D_MODEL, :], h, _NT,
                                     preferred_element_type=jnp.float32).astype(jnp.bfloat16)


def _meta_proj_call(meta, g, b, w_all, w_t):
    D = D_MODEL
    const = lambda i: (0, 0)
    full = lambda shape: pl.BlockSpec(shape, const)
    return pl.pallas_call(
        _meta_proj_kernel,
        grid=(1,),
        in_specs=[full((N_META, D)), full((1, D)), full((1, D)), _w_block(COL_KD, 1), _w_block(COL_KS, 1),
                  pl.BlockSpec((2 * D, D), lambda i: (1, 0))],
        out_specs=[full((N_META, D)), full((N_META, D)), full((D, N_META)), full((D, N_META))],
        out_shape=[jax.ShapeDtypeStruct((N_META, D), jnp.bfloat16)] * 2
        + [jax.ShapeDtypeStruct((D, N_META), jnp.bfloat16)] * 2,
        compiler_params=pltpu.CompilerParams(dimension_semantics=("arbitrary",), vmem_limit_bytes=VMEM_LIMIT),
        name="meta_proj",
    )(meta, g, b, w_all, w_all, w_t)


def _diff_kernel(slope_ref, lam_ref, qt_ref, k_ref, vt_ref, km_ref, vmt_ref, g_ref, o_ref,
                 s_scr, p_scr, acc_scr, m_scr, bias_scr, qn_scr, kpre_ref):
    nq = qt_ref.shape[0]
    slope2 = slope_ref[pl.program_id(1)] * LOG2E
    f32, bf16 = jnp.float32, jnp.bfloat16

    row = lax.broadcasted_iota(jnp.int32, (TD, TD), 0)
    col = lax.broadcasted_iota(jnp.int32, (TD, TD), 1)
    bias = slope2 * row.astype(f32)
    bias_scr[0] = bias
    bias_scr[1] = jnp.where(row <= col, bias, NEG_BIG)

    def with_ones(vt):
        return jnp.concatenate([vt, jnp.ones((ONES_ROWS, vt.shape[1]), vt.dtype)], axis=0)

    def split(qt):
        zero = jnp.zeros_like(qt)
        rw = lax.broadcasted_iota(jnp.int32, qt.shape, 0)
        return jnp.where(rw < DIFF_HEAD_DIM, qt, zero), jnp.where(rw >= DIFF_HEAD_DIM, qt, zero)

    S = nq * TD
    qt_all = jnp.concatenate([qt_ref[t] for t in range(nq)], axis=1)
    colq = lax.broadcasted_iota(jnp.int32, (N_META, S), 1)
    rowm = lax.broadcasted_iota(jnp.int32, (N_META, S), 0)
    bias_m = slope2 * (rowm - N_META - (colq - (colq & (TD - 1)))).astype(f32)
    vmt_ones = with_ones(vmt_ref[...])
    qsq = jnp.square(qt_all.astype(f32))
    for c, qc in enumerate(split(qt_all)):
        s = jnp.dot(km_ref[...], qc, preferred_element_type=f32) + bias_m
        m0 = jnp.max(s, axis=0, keepdims=True)
        p = jnp.exp2(s - m0)
        a0 = jnp.dot(vmt_ones, p.astype(bf16), preferred_element_type=f32)
        qn = jnp.sqrt(jnp.sum(qsq[c * DIFF_HEAD_DIM:(c + 1) * DIFF_HEAD_DIM], axis=0, keepdims=True))
        for t in range(nq):
            m_scr[t, c] = m0[:, t * TD:(t + 1) * TD]
            acc_scr[t, c] = a0[:, t * TD:(t + 1) * TD]
            qn_scr[t, c] = qn[:, t * TD:(t + 1) * TD]

    k_all = k_ref[...]
    ksq = k_all * k_all
    ln8 = lax.broadcasted_iota(jnp.int32, (8, LANES), 1)
    for c, half in enumerate((ln8 < DIFF_HEAD_DIM, ln8 >= DIFF_HEAD_DIM)):
        kn = jnp.sqrt(lax.dot_general(jnp.where(half, 1.0, 0.0).astype(bf16), ksq, _NT,
                                      preferred_element_type=f32)[0:1, :])
        kmax = jnp.float32(0.0)
        for t in range(nq):
            kmax = jnp.maximum(kmax, jnp.max(kn[:, t * TD:(t + 1) * TD]))
            kpre_ref[c * nq + t] = kmax

    def stage_a(qt, kt):
        k_tile = k_ref[pl.ds(pl.multiple_of(kt * TD, TD), TD), :]
        qcs = split(qt_ref[qt])
        tile_bias = bias_scr[(kt == qt).astype(jnp.int32)]
        mx = []
        for c in range(2):
            s = jnp.dot(k_tile, qcs[c], preferred_element_type=f32) + tile_bias
            s_scr[c] = s
            mx.append(jnp.max(s, axis=0, keepdims=True))
        return tuple(mx)

    def stage_b(qt, kt, mx):
        off2 = -(slope2 * TD) * (qt - kt).astype(f32)
        alphas, m_news = [], []
        for c in range(2):
            m_old = m_scr[qt, c]
            m_new = jnp.maximum(m_old, mx[c] + off2)
            alpha = jnp.exp2(m_old - m_new)
            p = jnp.exp2(s_scr[c] - (m_new - off2))
            m_scr[qt, c] = m_new
            p_scr[c] = p.astype(bf16)
            alphas.append(alpha)
            m_news.append(m_new)
        return tuple(alphas), tuple(m_news)

    def rest_is_zero(qt, kt, m_news):
        worst_bias = slope2 * (TD - 1) - (slope2 * TD) * (qt - kt).astype(f32)
        slack = None
        for c in range(2):
            bound = qn_scr[qt, c] * (kpre_ref[c * nq + kt] * NORM_SLACK) + (worst_bias + 1.0)
            gap = (m_news[c] - EXP2_ZERO) - bound
            slack = gap if slack is None else jnp.minimum(slack, gap)
        return jnp.min(slack) > 0.0

    def stage_c(qt, kt, alphas):
        vt_tile = with_ones(vt_ref[kt])
        for c in range(2):
            acc_scr[qt, c] = alphas[c] * acc_scr[qt, c] + jnp.dot(vt_tile, p_scr[c], preferred_element_type=f32)

    lam = (jnp.exp(jnp.sum(lam_ref[0:1, :] * lam_ref[1:2, :], axis=1, keepdims=True))
           - jnp.exp(jnp.sum(lam_ref[2:3, :] * lam_ref[3:4, :], axis=1, keepdims=True)) + LAM_INIT)

    def finalize(qt):
        num = [acc_scr[qt, c, 0:DIFF_V_DIM, :] for c in range(2)]
        den = [acc_scr[qt, c, DIFF_V_DIM:DIFF_V_DIM + 1, :] for c in range(2)]
        y = num[0] * (1.0 / den[0]) - lam * (num[1] * (1.0 / den[1]))
        ms = jnp.mean(y * y, axis=0, keepdims=True)
        y = y * lax.rsqrt(ms + RMS_EPS) * g_ref[...] * (1.0 - LAM_INIT)
        o_ref[pl.ds(pl.multiple_of(qt * TD, TD), TD), :] = y.T.astype(o_ref.dtype)

    p_scr[...] = jnp.zeros(p_scr.shape, bf16)
    ones = jnp.ones((1, TD), f32)
    zero_i = jnp.int32(0)
    mx0 = stage_a(zero_i, zero_i)

    def cond(st):
        return st[2] < nq

    def body(st):
        qa, ka, qb, kb, eb, qc, kc, ec, mx, alphas = st
        stage_c(qc, kc, alphas)
        alphas, m_news = stage_b(qb, kb, mx)
        qa_c, ka_c = jnp.minimum(qa, nq - 1), jnp.minimum(ka, nq - 1)
        ea = jnp.logical_or(ka == 0, jnp.logical_and(qa == qb, rest_is_zero(qa_c, jnp.maximum(ka_c - 1, 0), m_news)))
        mx = stage_a(qa_c, ka_c)

        @pl.when(ec)
        def _():
            finalize(qc)

        return (jnp.where(ea, qa + 1, qa), jnp.where(ea, qa + 1, ka - 1), qa, ka, ea, qb, kb, eb, mx, alphas)

    st = (jnp.int32(1), jnp.int32(1), zero_i, zero_i, jnp.bool_(True), zero_i, zero_i, jnp.bool_(False),
          mx0, (ones, ones))
    st = lax.while_loop(cond, body, st)
    _, _, _, _, _, qc, kc, _, _, alphas = st
    stage_c(qc, kc, alphas)
    finalize(qc)


def _diff_call(slopes, lam, qdt, kd, vdt, kdm, vdmt, g_col):
    B, S, D = kd.shape
    nq = S // TD
    smem = pl.BlockSpec(memory_space=pltpu.SMEM)
    slab = pl.BlockSpec((None, S, LANES), lambda b, h: (b, 0, h))
    tiles_t = pl.BlockSpec((None, nq, LANES, TD), lambda b, h: (b, 0, h, 0))
    return pl.pallas_call(
        _diff_kernel,
        grid=(B, DIFF_HEADS),
        in_specs=[smem, pl.BlockSpec((4, DIFF_HEAD_DIM), lambda b, h: (0, 0)),
                  tiles_t, slab, tiles_t,
                  pl.BlockSpec((N_META, LANES), lambda b, h: (0, h)),
                  pl.BlockSpec((DIFF_V_DIM, N_META), lambda b, h: (h, 0)),
                  pl.BlockSpec((DIFF_V_DIM, 1), lambda b, h: (0, 0))],
        out_specs=slab,
        out_shape=jax.ShapeDtypeStruct((B, S, D), jnp.bfloat16),
        scratch_shapes=[pltpu.VMEM((2, TD, TD), jnp.float32),
                        pltpu.VMEM((2, TD, TD), jnp.bfloat16),
                        pltpu.VMEM((nq, 2, DIFF_V_DIM + ONES_ROWS, TD), jnp.float32),
                        pltpu.VMEM((nq, 2, 1, TD), jnp.float32),
                        pltpu.VMEM((2, TD, TD), jnp.float32),
                        pltpu.VMEM((nq, 2, 1, TD), jnp.float32),
                        pltpu.SMEM((2 * nq,), jnp.float32)],
        compiler_params=pltpu.CompilerParams(dimension_semantics=("arbitrary",) * 2,
                                             vmem_limit_bytes=VMEM_LIMIT),
        name="diff_attn",
    )(slopes, lam, qdt, kd, vdt, kdm, vdmt, g_col)


def _sb_kernel(qt_ref, k_ref, vt_ref, km_ref, vmt_ref, u_ref, um_ref, o_ref, zi_scr, out_scr):
    nq = qt_ref.shape[0]
    f32, bf16 = jnp.float32, jnp.bfloat16
    u = u_ref[...]
    row = lax.broadcasted_iota(jnp.int32, (TS, TS), 0)
    col = lax.broadcasted_iota(jnp.int32, (TS, TS), 1)
    strict = row < col

    def rows(i):
        return pl.ds(pl.multiple_of(i * TS, TS), TS)

    def head_queries(i):
        qt = qt_ref[i]
        rw = lax.broadcasted_iota(jnp.int32, qt.shape, 0)
        zero = jnp.zeros_like(qt)
        return [jnp.where(jnp.logical_and(rw >= hh * SB_HEAD_DIM, rw < (hh + 1) * SB_HEAD_DIM), qt, zero)
                for hh in range(SB_HP)]

    def head_rows(hh):
        return slice(hh * SB_HEAD_DIM, (hh + 1) * SB_HEAD_DIM)

    def suffix_logits(k_tile, qh, mask, umat):
        z = jnp.dot(k_tile, qh, preferred_element_type=f32)
        e = jnp.exp2(jnp.abs(z) * (-LOG2E))
        sp = jnp.maximum(z, 0.0) + jnp.log(1.0 + e)
        if mask is not None:
            sp = jnp.where(mask, sp, 0.0)
        incl = jnp.dot(umat, sp.astype(bf16), preferred_element_type=f32)
        return z + incl, incl[0:1, :]

    def stage_a(i):
        qhs = head_queries(i)
        il = jnp.maximum(i - 1, 0)
        k_d, k_l = k_ref[rows(i), :], k_ref[rows(il), :]
        tots = []
        for ch in range(2 * SB_HP):
            diag = ch < SB_HP
            zi, tot = suffix_logits(k_d if diag else k_l, qhs[ch % SB_HP], strict if diag else None, u)
            zi_scr[ch] = zi
            tots.append(tot)
        return tuple(tots)

    def stage_b(i, tots):
        has_left = i > 0
        il = jnp.maximum(i - 1, 0)
        carries = []
        for hh in range(SB_HP):
            w_d = jnp.where(strict, jnp.exp(zi_scr[hh]), 0.0).astype(bf16)
            w_l = jnp.exp(zi_scr[SB_HP + hh] + jnp.where(has_left, tots[hh], NEG_BIG)).astype(bf16)
            out_scr[head_rows(hh), :] = (jnp.dot(vt_ref[i, head_rows(hh), :], w_d, preferred_element_type=f32)
                                         + jnp.dot(vt_ref[il, head_rows(hh), :], w_l, preferred_element_type=f32))
            carries.append(tots[hh] + jnp.where(has_left, tots[SB_HP + hh], 0.0))
        return carries

    def further_tiles(i, carries):
        qhs = head_queries(i)
        for hh in range(SB_HP):

            def tile(k_tile, v_tile, umat, carry, acc):
                zi, tot = suffix_logits(k_tile, qhs[hh], None, umat)
                w = jnp.exp(zi + carry).astype(bf16)
                return carry + tot, acc + jnp.dot(v_tile, w, preferred_element_type=f32)

            def cond(st):
                j, cmax, _, _ = st
                return jnp.logical_and(j >= 0, cmax > SB_EXIT_LOG)

            def body(st):
                j, _, carry, acc = st
                carry, acc = tile(k_ref[rows(j), :], vt_ref[j, head_rows(hh), :], u, carry, acc)
                return j - 1, jnp.max(carry), carry, acc

            st = (i - 2, jnp.max(carries[hh]), carries[hh], out_scr[head_rows(hh), :])
            _, cmax, carry, acc = lax.while_loop(cond, body, st)

            def meta(acc):
                return tile(km_ref[...], vmt_ref[head_rows(hh), :], um_ref[...], carry, acc)[1]

            out_scr[head_rows(hh), :] = lax.cond(cmax > SB_EXIT_LOG, meta, lambda a: a, acc)

    def write_out(i):
        o_ref[rows(i), :] = out_scr[...].T.astype(o_ref.dtype)

    zi_scr[...] = jnp.zeros(zi_scr.shape, f32)
    out_scr[...] = jnp.zeros(out_scr.shape, f32)
    zero_row = jnp.zeros((1, TS), f32)

    def body(i, tots):
        ib = jnp.maximum(i - 1, 0)
        write_out(jnp.maximum(i - 2, 0))
        carries = stage_b(ib, tots)
        tots = stage_a(jnp.minimum(i, nq - 1))
        worst = functools.reduce(jnp.maximum, carries)

        @pl.when(jnp.logical_and(i >= 1, jnp.max(worst) > SB_EXIT_LOG))
        def _():
            further_tiles(ib, carries)

        return tots

    lax.fori_loop(0, nq + 1, body, (zero_row,) * (2 * SB_HP))
    write_out(nq - 1)


def _sb_call(qst, ks, vst, ksm, vsmt, u, um):
    B, S, D = ks.shape
    width = SB_HP * SB_HEAD_DIM
    const = lambda b, h: (0, 0)
    slab = pl.BlockSpec((None, S, width), lambda b, h: (b, 0, h))
    tiles_t = pl.BlockSpec((None, S // TS, width, TS), lambda b, h: (b, 0, h, 0))
    return pl.pallas_call(
        _sb_kernel,
        grid=(B, SB_HEADS // SB_HP),
        in_specs=[tiles_t, slab, tiles_t,
                  pl.BlockSpec((N_META, width), lambda b, h: (0, h)),
                  pl.BlockSpec((width, N_META), lambda b, h: (h, 0)),
                  pl.BlockSpec((TS, TS), const), pl.BlockSpec((N_META, N_META), const)],
        out_specs=slab,
        out_shape=jax.ShapeDtypeStruct((B, S, D), jnp.bfloat16),
        scratch_shapes=[pltpu.VMEM((2 * SB_HP, TS, TS), jnp.float32),
                        pltpu.VMEM((width, TS), jnp.float32)],
        compiler_params=pltpu.CompilerParams(dimension_semantics=("arbitrary",) * 2,
                                             vmem_limit_bytes=VMEM_LIMIT),
        name="sb_attn",
    )(qst, ks, vst, ksm, vsmt, u, um)


def _sigmoid(x):
    return 1.0 / (1.0 + jnp.exp(-x))


def _final_kernel(x_ref, yd_ref, ys_ref, eg_ref, eb_ref, wgd_ref, wgs_ref, wmd_ref, wms_ref,
                  wbd_ref, wbs_ref, wo_ref, bg_ref, lg_ref, lb_ref, o_ref):
    for r in range(TM_FINAL // FINAL_CHUNK):
        rs = slice(r * FINAL_CHUNK, (r + 1) * FINAL_CHUNK)
        h = _layer_norm(x_ref[rs, :], eg_ref[...], eb_ref[...])
        hb = h.astype(jnp.bfloat16)

        def proj(w_ref):
            return jnp.dot(hb, w_ref[...], preferred_element_type=jnp.float32)

        g_d = proj(wgd_ref)
        u_d = (yd_ref[rs, :].astype(jnp.float32) * (g_d * _sigmoid(g_d))).astype(jnp.bfloat16)
        br_d = jnp.dot(u_d, wbd_ref[...], preferred_element_type=jnp.float32)
        merged = _sigmoid(proj(wmd_ref) + bg_ref[0:1, :]) * br_d
        g_s = proj(wgs_ref)
        u_s = (ys_ref[rs, :].astype(jnp.float32) * (g_s * _sigmoid(g_s))).astype(jnp.bfloat16)
        br_s = jnp.dot(u_s, wbs_ref[...], preferred_element_type=jnp.float32)
        merged = merged + _sigmoid(proj(wms_ref) + bg_ref[1:2, :]) * br_s
        y = jnp.dot(merged.astype(jnp.bfloat16), wo_ref[...], preferred_element_type=jnp.float32)
        o_ref[rs, :] = _layer_norm(DN_ALPHA * h + y, lg_ref[...], lb_ref[...]).astype(o_ref.dtype)


def _final_call(x2, yd2, ys2, eg, eb, w_all, wbd, wbs, wo, bg, lg, lb):
    R, D = x2.shape
    row = pl.BlockSpec((TM_FINAL, D), lambda i: (i, 0))
    const = lambda i: (0, 0)
    vec = pl.BlockSpec((1, D), const)
    weight = pl.BlockSpec((D, D), const, pipeline_mode=pl.Buffered(1))
    return pl.pallas_call(
        _final_kernel,
        grid=(R // TM_FINAL,),
        in_specs=[row, row, row, vec, vec]
        + [_w_block(c, 1) for c in (COL_GD, COL_GS, COL_MD, COL_MS)]
        + [weight, weight, weight, pl.BlockSpec((2, D), const), vec, vec],
        out_specs=row,
        out_shape=jax.ShapeDtypeStruct((R, D), jnp.float32),
        compiler_params=pltpu.CompilerParams(dimension_semantics=("arbitrary",),
                                             vmem_limit_bytes=VMEM_LIMIT),
        name="final",
    )(x2, yd2, ys2, eg, eb, w_all, w_all, w_all, w_all, wbd, wbs, wo, bg, lg, lb)


def _neg_upper_ones(n):
    r = jnp.arange(n)
    return jnp.where(r[None, :] >= r[:, None], -1.0, 0.0).astype(jnp.bfloat16)


def kernel(x, meta_tokens, emb_ln_g, emb_ln_b, w_in, b_gate, diff_lambda, diff_subln_g,
           w_br_diff, w_br_sb, w_out, ln_g, ln_b):
    B, S, D = x.shape
    bf16 = jnp.bfloat16
    qscale = DIFF_HEAD_DIM ** -0.5
    col_scale = np.ones((N_COLS * D,), np.float32)
    col_scale[COL_QD * D:(COL_QD + 1) * D] = qscale * LOG2E
    col_scale[COL_QS * D:(COL_QS + 1) * D] = qscale
    w_all = (w_in[0] * col_scale).astype(bf16)
    w_t = jnp.concatenate([w_all[:, c * D:(c + 1) * D].T for c in (COL_QD, COL_QS, COL_VD, COL_VS)], axis=0)
    eg, eb = emb_ln_g.reshape(1, D), emb_ln_b.reshape(1, D)

    kd, ks, qdt, qst, vdt, vst = _proj_call(x, eg, eb, w_all, w_t)
    kd_meta, ks_meta, vdt_meta, vst_meta = _meta_proj_call(meta_tokens, eg, eb, w_all, w_t)

    slopes = 2.0 ** (-8.0 * jnp.arange(1, DIFF_HEADS + 1, dtype=jnp.float32) / DIFF_HEADS)

    yd = _diff_call(slopes, diff_lambda[0], qdt, kd, vdt, kd_meta, vdt_meta, diff_subln_g[0].reshape(DIFF_V_DIM, 1))
    ys = _sb_call(qst, ks, vst, ks_meta, vst_meta, _neg_upper_ones(TS), _neg_upper_ones(N_META))

    out = _final_call(x.reshape(B * S, D), yd.reshape(B * S, D), ys.reshape(B * S, D), eg, eb,
                      w_all, w_br_diff[0].astype(bf16), w_br_sb[0].astype(bf16), w_out[0].astype(bf16),
                      b_gate[0], ln_g[0].reshape(1, D), ln_b[0].reshape(1, D))
    return out.reshape(B, S, D)
```

```python
import functools
import math

import jax
import jax.numpy as jnp
import numpy as np
from jax import lax
from jax.experimental import pallas as pl
from jax.experimental.pallas import tpu as pltpu

D_MODEL = 1024
N_META = 16
DIFF_HEADS = 8
DIFF_HEAD_DIM = 64
DIFF_V_DIM = 128
SB_HEADS = 16
SB_HEAD_DIM = 64
DEPTH = 1
DN_ALPHA = (2.0 * DEPTH) ** 0.25
LN_EPS = 1e-5
RMS_EPS = 1e-5
LAM_INIT = 0.8 - 0.6 * math.exp(-0.3 * 0)

COL_QD, COL_KD, COL_VD, COL_GD, COL_QS, COL_KS, COL_VS, COL_GS, COL_MD, COL_MS = range(10)
N_COLS = 10

LANES = 128
TD = 512
ONES_ROWS = 16
TS = 256
SB_HP = 4
TM_PROJ = 512
TM_FINAL = 512
FINAL_CHUNK = 256
VMEM_LIMIT = 56 * 1024 * 1024
LOG2E = 1.4426950408889634
NEG_BIG = -1e30
SB_EXIT_LOG = -105.0
EXP2_ZERO = 150.0
NORM_SLACK = 1.01
SCORE_MARGIN = 1.0

_NT = (((1,), (1,)), ((), ()))


def _layer_norm(x, g, b):
    mu = jnp.mean(x, axis=-1, keepdims=True)
    xc = x - mu
    var = jnp.mean(xc * xc, axis=-1, keepdims=True)
    return xc * lax.rsqrt(var + LN_EPS) * g + b


def _proj_kernel(x_ref, g_ref, b_ref, wkd_ref, wks_ref, wt_ref,
                 kd_ref, ks_ref, qdt_ref, qst_ref, vdt_ref, vst_ref):
    h = _layer_norm(x_ref[...], g_ref[...], b_ref[...]).astype(jnp.bfloat16)
    for w_ref, o_ref in ((wkd_ref, kd_ref), (wks_ref, ks_ref)):
        o_ref[...] = jnp.dot(h, w_ref[...], preferred_element_type=jnp.float32).astype(jnp.bfloat16)
    for c, (o_ref, tk) in enumerate(((qdt_ref, TD), (qst_ref, TS), (vdt_ref, TD), (vst_ref, TS))):
        xt = lax.dot_general(wt_ref[c * D_MODEL:(c + 1) * D_MODEL, :], h, _NT,
                             preferred_element_type=jnp.float32).astype(jnp.bfloat16)
        for t in range(TM_PROJ // tk):
            o_ref[t] = xt[:, t * tk:(t + 1) * tk]


def _w_block(c, grid_rank):
    index_map = (lambda i: (0, c)) if grid_rank == 1 else (lambda bi, i: (0, c))
    return pl.BlockSpec((D_MODEL, D_MODEL), index_map, pipeline_mode=pl.Buffered(1))


def _proj_call(x, g, b, w_all, w_t):
    B, S, D = x.shape
    nt = S // TM_PROJ
    row_spec = pl.BlockSpec((None, TM_PROJ, D), lambda bi, i: (bi, i, 0))
    tiles = (TD, TS, TD, TS)
    t_specs = [pl.BlockSpec((None, TM_PROJ // tk, D, tk), lambda bi, i: (bi, i, 0, 0)) for tk in tiles]
    const = lambda bi, i: (0, 0)
    return pl.pallas_call(
        _proj_kernel,
        grid=(B, nt),
        in_specs=[row_spec,
                  pl.BlockSpec((1, D), const), pl.BlockSpec((1, D), const),
                  _w_block(COL_KD, 2), _w_block(COL_KS, 2),
                  pl.BlockSpec((4 * D, D), const, pipeline_mode=pl.Buffered(1))],
        out_specs=[row_spec, row_spec] + t_specs,
        out_shape=[jax.ShapeDtypeStruct((B, S, D), jnp.bfloat16)] * 2
        + [jax.ShapeDtypeStruct((B, S // tk, D, tk), jnp.bfloat16) for tk in tiles],
        compiler_params=pltpu.CompilerParams(dimension_semantics=("arbitrary", "arbitrary"),
                                             vmem_limit_bytes=VMEM_LIMIT),
        name="proj",
    )(x, g, b, w_all, w_all, w_t)


def _meta_proj_kernel(m_ref, g_ref, b_ref, wkd_ref, wks_ref, wvt_ref, kd_ref, ks_ref, vdt_ref, vst_ref):
    h = _layer_norm(m_ref[...], g_ref[...], b_ref[...]).astype(jnp.bfloat16)
    for w_ref, o_ref in ((wkd_ref, kd_ref), (wks_ref, ks_ref)):
        o_ref[...] = jnp.dot(h, w_ref[...], preferred_element_type=jnp.float32).astype(jnp.bfloat16)
    for c, o_ref in enumerate((vdt_ref, vst_ref)):
        o_ref[...] = lax.dot_general(wvt_ref[c * D_MODEL:(c + 1) * D_MODEL, :], h, _NT,
                                     preferred_element_type=jnp.float32).astype(jnp.bfloat16)


def _meta_proj_call(meta, g, b, w_all, w_t):
    D = D_MODEL
    const = lambda i: (0, 0)
    full = lambda shape: pl.BlockSpec(shape, const)
    return pl.pallas_call(
        _meta_proj_kernel,
        grid=(1,),
        in_specs=[full((N_META, D)), full((1, D)), full((1, D)), _w_block(COL_KD, 1), _w_block(COL_KS, 1),
                  pl.BlockSpec((2 * D, D), lambda i: (1, 0))],
        out_specs=[full((N_META, D)), full((N_META, D)), full((D, N_META)), full((D, N_META))],
        out_shape=[jax.ShapeDtypeStruct((N_META, D), jnp.bfloat16)] * 2
        + [jax.ShapeDtypeStruct((D, N_META), jnp.bfloat16)] * 2,
        compiler_params=pltpu.CompilerParams(dimension_semantics=("arbitrary",), vmem_limit_bytes=VMEM_LIMIT),
        name="meta_proj",
    )(meta, g, b, w_all, w_all, w_t)


def _diff_kernel(slope_ref, lam_ref, qt_ref, k_ref, vt_ref, km_ref, vmt_ref, g_ref, o_ref,
                 s_scr, p_scr, acc_scr, m_scr, bias_scr, qn_scr, kpre_ref):
    nq = qt_ref.shape[0]
    slope2 = slope_ref[pl.program_id(1)] * LOG2E
    f32, bf16 = jnp.float32, jnp.bfloat16

    row = lax.broadcasted_iota(jnp.int32, (TD, TD), 0)
    col = lax.broadcasted_iota(jnp.int32, (TD, TD), 1)
    bias = slope2 * row.astype(f32)
    bias_scr[0] = bias
    bias_scr[1] = jnp.where(row <= col, bias, NEG_BIG)

    def with_ones(vt):
        return jnp.concatenate([vt, jnp.ones((ONES_ROWS, vt.shape[1]), vt.dtype)], axis=0)

    def split(qt):
        zero = jnp.zeros_like(qt)
        rw = lax.broadcasted_iota(jnp.int32, qt.shape, 0)
        return jnp.where(rw < DIFF_HEAD_DIM, qt, zero), jnp.where(rw >= DIFF_HEAD_DIM, qt, zero)

    S = nq * TD
    qt_all = jnp.concatenate([qt_ref[t] for t in range(nq)], axis=1)
    colq = lax.broadcasted_iota(jnp.int32, (N_META, S), 1)
    rowm = lax.broadcasted_iota(jnp.int32, (N_META, S), 0)
    bias_m = slope2 * (rowm - N_META - (colq - (colq & (TD - 1)))).astype(f32)
    vmt_ones = with_ones(vmt_ref[...])
    qsq = jnp.square(qt_all.astype(f32))
    for c, qc in enumerate(split(qt_all)):
        s = jnp.dot(km_ref[...], qc, preferred_element_type=f32) + bias_m
        m0 = jnp.max(s, axis=0, keepdims=True)
        p = jnp.exp2(s - m0)
        a0 = jnp.dot(vmt_ones, p.astype(bf16), preferred_element_type=f32)
        qn = jnp.sqrt(jnp.sum(qsq[c * DIFF_HEAD_DIM:(c + 1) * DIFF_HEAD_DIM], axis=0, keepdims=True))
        for t in range(nq):
            m_scr[t, c] = m0[:, t * TD:(t + 1) * TD]
            acc_scr[t, c] = a0[:, t * TD:(t + 1) * TD]
            qn_scr[t, c] = qn[:, t * TD:(t + 1) * TD]

    k_all = k_ref[...]
    ksq = k_all * k_all
    ln8 = lax.broadcasted_iota(jnp.int32, (8, LANES), 1)
    for c, half in enumerate((ln8 < DIFF_HEAD_DIM, ln8 >= DIFF_HEAD_DIM)):
        kn = jnp.sqrt(lax.dot_general(jnp.where(half, 1.0, 0.0).astype(bf16), ksq, _NT,
                                      preferred_element_type=f32)[0:1, :])
        kmax = jnp.float32(0.0)
        for t in range(nq):
            kmax = jnp.maximum(kmax, jnp.max(kn[:, t * TD:(t + 1) * TD]))
            kpre_ref[c * nq + t] = kmax

    def stage_a(qt, kt):
        k_tile = k_ref[pl.ds(pl.multiple_of(kt * TD, TD), TD), :]
        qcs = split(qt_ref[qt])
        tile_bias = bias_scr[(kt == qt).astype(jnp.int32)]
        mx = []
        for c in range(2):
            s = jnp.dot(k_tile, qcs[c], preferred_element_type=f32) + tile_bias
            s_scr[c] = s
            mx.append(jnp.max(s, axis=0, keepdims=True))
        return tuple(mx)

    def stage_b(qt, kt, mx):
        off2 = -(slope2 * TD) * (qt - kt).astype(f32)
        alphas, m_news = [], []
        for c in range(2):
            m_old = m_scr[qt, c]
            m_new = jnp.maximum(m_old, mx[c] + off2)
            alpha = jnp.exp2(m_old - m_new)
            p = jnp.exp2(s_scr[c] - (m_new - off2))
            m_scr[qt, c] = m_new
            p_scr[c] = p.astype(bf16)
            alphas.append(alpha)
            m_news.append(m_new)
        return tuple(alphas), tuple(m_news)

    def rest_is_zero(qt, kt, m_news):
        worst_bias = slope2 * (TD - 1) - (slope2 * TD) * (qt - kt).astype(f32)
        slack = None
        for c in range(2):
            bound = qn_scr[qt, c] * (kpre_ref[c * nq + kt] * NORM_SLACK) + (worst_bias + SCORE_MARGIN)
            gap = (m_news[c] - EXP2_ZERO) - bound
            slack = gap if slack is None else jnp.minimum(slack, gap)
        return jnp.min(slack) > 0.0

    def stage_c(qt, kt, alphas):
        vt_tile = with_ones(vt_ref[kt])
        for c in range(2):
            acc_scr[qt, c] = alphas[c] * acc_scr[qt, c] + jnp.dot(vt_tile, p_scr[c], preferred_element_type=f32)

    lam = (jnp.exp(jnp.sum(lam_ref[0:1, :] * lam_ref[1:2, :], axis=1, keepdims=True))
           - jnp.exp(jnp.sum(lam_ref[2:3, :] * lam_ref[3:4, :], axis=1, keepdims=True)) + LAM_INIT)

    def finalize(qt):
        num = [acc_scr[qt, c, 0:DIFF_V_DIM, :] for c in range(2)]
        den = [acc_scr[qt, c, DIFF_V_DIM:DIFF_V_DIM + 1, :] for c in range(2)]
        y = num[0] * (1.0 / den[0]) - lam * (num[1] * (1.0 / den[1]))
        ms = jnp.mean(y * y, axis=0, keepdims=True)
        y = y * lax.rsqrt(ms + RMS_EPS) * g_ref[...] * (1.0 - LAM_INIT)
        o_ref[pl.ds(pl.multiple_of(qt * TD, TD), TD), :] = y.T.astype(o_ref.dtype)

    p_scr[...] = jnp.zeros(p_scr.shape, bf16)
    ones = jnp.ones((1, TD), f32)
    zero_i = jnp.int32(0)
    mx0 = stage_a(zero_i, zero_i)

    def cond(st):
        return st[2] < nq

    def body(st):
        qa, ka, qb, kb, eb, qc, kc, ec, mx, alphas = st
        stage_c(qc, kc, alphas)
        alphas, m_news = stage_b(qb, kb, mx)
        qa_c, ka_c = jnp.minimum(qa, nq - 1), jnp.minimum(ka, nq - 1)
        ea = jnp.logical_or(ka == 0, jnp.logical_and(qa == qb, rest_is_zero(qa_c, jnp.maximum(ka_c - 1, 0), m_news)))
        mx = stage_a(qa_c, ka_c)

        @pl.when(ec)
        def _():
            finalize(qc)

        return (jnp.where(ea, qa + 1, qa), jnp.where(ea, qa + 1, ka - 1), qa, ka, ea, qb, kb, eb, mx, alphas)

    st = (jnp.int32(1), jnp.int32(1), zero_i, zero_i, jnp.bool_(True), zero_i, zero_i, jnp.bool_(False),
          mx0, (ones, ones))
    st = lax.while_loop(cond, body, st)
    _, _, _, _, _, qc, kc, _, _, alphas = st
    stage_c(qc, kc, alphas)
    finalize(qc)


def _diff_call(slopes, lam, qdt, kd, vdt, kdm, vdmt, g_col):
    B, S, D = kd.shape
    nq = S // TD
    smem = pl.BlockSpec(memory_space=pltpu.SMEM)
    slab = pl.BlockSpec((None, S, LANES), lambda b, h: (b, 0, h))
    tiles_t = pl.BlockSpec((None, nq, LANES, TD), lambda b, h: (b, 0, h, 0))
    return pl.pallas_call(
        _diff_kernel,
        grid=(B, DIFF_HEADS),
        in_specs=[smem, pl.BlockSpec((4, DIFF_HEAD_DIM), lambda b, h: (0, 0)),
                  tiles_t, slab, tiles_t,
                  pl.BlockSpec((N_META, LANES), lambda b, h: (0, h)),
                  pl.BlockSpec((DIFF_V_DIM, N_META), lambda b, h: (h, 0)),
                  pl.BlockSpec((DIFF_V_DIM, 1), lambda b, h: (0, 0))],
        out_specs=slab,
        out_shape=jax.ShapeDtypeStruct((B, S, D), jnp.bfloat16),
        scratch_shapes=[pltpu.VMEM((2, TD, TD), jnp.float32),
                        pltpu.VMEM((2, TD, TD), jnp.bfloat16),
                        pltpu.VMEM((nq, 2, DIFF_V_DIM + ONES_ROWS, TD), jnp.float32),
                        pltpu.VMEM((nq, 2, 1, TD), jnp.float32),
                        pltpu.VMEM((2, TD, TD), jnp.float32),
                        pltpu.VMEM((nq, 2, 1, TD), jnp.float32),
                        pltpu.SMEM((2 * nq,), jnp.float32)],
        compiler_params=pltpu.CompilerParams(dimension_semantics=("arbitrary",) * 2,
                                             vmem_limit_bytes=VMEM_LIMIT),
        name="diff_attn",
    )(slopes, lam, qdt, kd, vdt, kdm, vdmt, g_col)


def _sb_kernel(qt_ref, k_ref, vt_ref, km_ref, vmt_ref, u_ref, um_ref, o_ref, zi_scr, out_scr):
    nq = qt_ref.shape[0]
    f32, bf16 = jnp.float32, jnp.bfloat16
    u = u_ref[...]
    row = lax.broadcasted_iota(jnp.int32, (TS, TS), 0)
    col = lax.broadcasted_iota(jnp.int32, (TS, TS), 1)
    strict = row < col

    def rows(i):
        return pl.ds(pl.multiple_of(i * TS, TS), TS)

    def lane_tile(hh):
        lt = hh * SB_HEAD_DIM // LANES
        return slice(lt * LANES, (lt + 1) * LANES)

    def head_queries(i):
        rw = lax.broadcasted_iota(jnp.int32, (LANES, TS), 0)
        out = []
        for hh in range(SB_HP):
            slab = qt_ref[i, lane_tile(hh), :]
            lo = hh * SB_HEAD_DIM % LANES
            out.append(jnp.where(jnp.logical_and(rw >= lo, rw < lo + SB_HEAD_DIM), slab, jnp.zeros_like(slab)))
        return out

    def head_rows(hh):
        return slice(hh * SB_HEAD_DIM, (hh + 1) * SB_HEAD_DIM)

    def suffix_logits(k_tile, qh, mask, umat):
        z = jnp.dot(k_tile, qh, preferred_element_type=f32)
        e = jnp.exp2(jnp.abs(z) * (-LOG2E))
        sp = jnp.maximum(z, 0.0) + jnp.log(1.0 + e)
        if mask is not None:
            sp = jnp.where(mask, sp, 0.0)
        incl = jnp.dot(umat, sp.astype(bf16), preferred_element_type=f32)
        return z + incl, incl[0:1, :]

    def stage_a(i):
        qhs = head_queries(i)
        il = jnp.maximum(i - 1, 0)
        k_d, k_l = k_ref[rows(i), :], k_ref[rows(il), :]
        tots = []
        for ch in range(2 * SB_HP):
            diag, hh = ch < SB_HP, ch % SB_HP
            zi, tot = suffix_logits((k_d if diag else k_l)[:, lane_tile(hh)], qhs[hh], strict if diag else None, u)
            zi_scr[ch] = zi
            tots.append(tot)
        return tuple(tots)

    def stage_b(i, tots):
        has_left = i > 0
        il = jnp.maximum(i - 1, 0)
        carries = []
        for hh in range(SB_HP):
            w_d = jnp.where(strict, jnp.exp(zi_scr[hh]), 0.0).astype(bf16)
            w_l = jnp.exp(zi_scr[SB_HP + hh] + jnp.where(has_left, tots[hh], NEG_BIG)).astype(bf16)
            out_scr[head_rows(hh), :] = (jnp.dot(vt_ref[i, head_rows(hh), :], w_d, preferred_element_type=f32)
                                         + jnp.dot(vt_ref[il, head_rows(hh), :], w_l, preferred_element_type=f32))
            carries.append(tots[hh] + jnp.where(has_left, tots[SB_HP + hh], 0.0))
        return carries

    def further_tiles(i, carries):
        qhs = head_queries(i)
        for hh in range(SB_HP):

            def tile(k_tile, v_tile, umat, carry, acc):
                zi, tot = suffix_logits(k_tile, qhs[hh], None, umat)
                w = jnp.exp(zi + carry).astype(bf16)
                return carry + tot, acc + jnp.dot(v_tile, w, preferred_element_type=f32)

            def cond(st):
                j, cmax, _, _ = st
                return jnp.logical_and(j >= 0, cmax > SB_EXIT_LOG)

            def body(st):
                j, _, carry, acc = st
                carry, acc = tile(k_ref[rows(j), lane_tile(hh)], vt_ref[j, head_rows(hh), :], u, carry, acc)
                return j - 1, jnp.max(carry), carry, acc

            st = (i - 2, jnp.max(carries[hh]), carries[hh], out_scr[head_rows(hh), :])
            _, cmax, carry, acc = lax.while_loop(cond, body, st)

            def meta(acc):
                return tile(km_ref[:, lane_tile(hh)], vmt_ref[head_rows(hh), :], um_ref[...], carry, acc)[1]

            out_scr[head_rows(hh), :] = lax.cond(cmax > SB_EXIT_LOG, meta, lambda a: a, acc)

    def write_out(i):
        o_ref[rows(i), :] = out_scr[...].T.astype(o_ref.dtype)

    zi_scr[...] = jnp.zeros(zi_scr.shape, f32)
    out_scr[...] = jnp.zeros(out_scr.shape, f32)
    zero_row = jnp.zeros((1, TS), f32)

    def body(i, tots):
        ib = jnp.maximum(i - 1, 0)
        write_out(jnp.maximum(i - 2, 0))
        carries = stage_b(ib, tots)
        tots = stage_a(jnp.minimum(i, nq - 1))
        worst = functools.reduce(jnp.maximum, carries)

        @pl.when(jnp.logical_and(i >= 1, jnp.max(worst) > SB_EXIT_LOG))
        def _():
            further_tiles(ib, carries)

        return tots

    lax.fori_loop(0, nq + 1, body, (zero_row,) * (2 * SB_HP))
    write_out(nq - 1)


def _sb_call(qst, ks, vst, ksm, vsmt, u, um):
    B, S, D = ks.shape
    width = SB_HP * SB_HEAD_DIM
    const = lambda b, h: (0, 0)
    slab = pl.BlockSpec((None, S, width), lambda b, h: (b, 0, h))
    tiles_t = pl.BlockSpec((None, S // TS, width, TS), lambda b, h: (b, 0, h, 0))
    return pl.pallas_call(
        _sb_kernel,
        grid=(B, SB_HEADS // SB_HP),
        in_specs=[tiles_t, slab, tiles_t,
                  pl.BlockSpec((N_META, width), lambda b, h: (0, h)),
                  pl.BlockSpec((width, N_META), lambda b, h: (h, 0)),
                  pl.BlockSpec((TS, TS), const), pl.BlockSpec((N_META, N_META), const)],
        out_specs=slab,
        out_shape=jax.ShapeDtypeStruct((B, S, D), jnp.bfloat16),
        scratch_shapes=[pltpu.VMEM((2 * SB_HP, TS, TS), jnp.float32),
                        pltpu.VMEM((width, TS), jnp.float32)],
        compiler_params=pltpu.CompilerParams(dimension_semantics=("arbitrary",) * 2,
                                             vmem_limit_bytes=VMEM_LIMIT),
        name="sb_attn",
    )(qst, ks, vst, ksm, vsmt, u, um)


def _sigmoid(x):
    return 1.0 / (1.0 + jnp.exp(-x))


def _final_kernel(x_ref, yd_ref, ys_ref, eg_ref, eb_ref, wgd_ref, wgs_ref, wmd_ref, wms_ref,
                  wbd_ref, wbs_ref, wo_ref, bg_ref, lg_ref, lb_ref, o_ref):
    for r in range(TM_FINAL // FINAL_CHUNK):
        rs = slice(r * FINAL_CHUNK, (r + 1) * FINAL_CHUNK)
        h = _layer_norm(x_ref[rs, :], eg_ref[...], eb_ref[...])
        hb = h.astype(jnp.bfloat16)

        def proj(w_ref):
            return jnp.dot(hb, w_ref[...], preferred_element_type=jnp.float32)

        g_d = proj(wgd_ref)
        u_d = (yd_ref[rs, :].astype(jnp.float32) * (g_d * _sigmoid(g_d))).astype(jnp.bfloat16)
        br_d = jnp.dot(u_d, wbd_ref[...], preferred_element_type=jnp.float32)
        merged = _sigmoid(proj(wmd_ref) + bg_ref[0:1, :]) * br_d
        g_s = proj(wgs_ref)
        u_s = (ys_ref[rs, :].astype(jnp.float32) * (g_s * _sigmoid(g_s))).astype(jnp.bfloat16)
        br_s = jnp.dot(u_s, wbs_ref[...], preferred_element_type=jnp.float32)
        merged = merged + _sigmoid(proj(wms_ref) + bg_ref[1:2, :]) * br_s
        y = jnp.dot(merged.astype(jnp.bfloat16), wo_ref[...], preferred_element_type=jnp.float32)
        o_ref[rs, :] = _layer_norm(DN_ALPHA * h + y, lg_ref[...], lb_ref[...]).astype(o_ref.dtype)


def _final_call(x2, yd2, ys2, eg, eb, w_all, wbd, wbs, wo, bg, lg, lb):
    R, D = x2.shape
    row = pl.BlockSpec((TM_FINAL, D), lambda i: (i, 0))
    const = lambda i: (0, 0)
    vec = pl.BlockSpec((1, D), const)
    weight = pl.BlockSpec((D, D), const, pipeline_mode=pl.Buffered(1))
    return pl.pallas_call(
        _final_kernel,
        grid=(R // TM_FINAL,),
        in_specs=[row, row, row, vec, vec]
        + [_w_block(c, 1) for c in (COL_GD, COL_GS, COL_MD, COL_MS)]
        + [weight, weight, weight, pl.BlockSpec((2, D), const), vec, vec],
        out_specs=row,
        out_shape=jax.ShapeDtypeStruct((R, D), jnp.float32),
        compiler_params=pltpu.CompilerParams(dimension_semantics=("arbitrary",),
                                             vmem_limit_bytes=VMEM_LIMIT),
        name="final",
    )(x2, yd2, ys2, eg, eb, w_all, w_all, w_all, w_all, wbd, wbs, wo, bg, lg, lb)


def _neg_upper_ones(n):
    r = jnp.arange(n)
    return jnp.where(r[None, :] >= r[:, None], -1.0, 0.0).astype(jnp.bfloat16)


def kernel(x, meta_tokens, emb_ln_g, emb_ln_b, w_in, b_gate, diff_lambda, diff_subln_g,
           w_br_diff, w_br_sb, w_out, ln_g, ln_b):
    B, S, D = x.shape
    bf16 = jnp.bfloat16
    qscale = DIFF_HEAD_DIM ** -0.5
    col_scale = np.ones((N_COLS * D,), np.float32)
    col_scale[COL_QD * D:(COL_QD + 1) * D] = qscale * LOG2E
    col_scale[COL_QS * D:(COL_QS + 1) * D] = qscale
    w_all = (w_in[0] * col_scale).astype(bf16)
    w_t = jnp.concatenate([w_all[:, c * D:(c + 1) * D].T for c in (COL_QD, COL_QS, COL_VD, COL_VS)], axis=0)
    eg, eb = emb_ln_g.reshape(1, D), emb_ln_b.reshape(1, D)

    kd, ks, qdt, qst, vdt, vst = _proj_call(x, eg, eb, w_all, w_t)
    kd_meta, ks_meta, vdt_meta, vst_meta = _meta_proj_call(meta_tokens, eg, eb, w_all, w_t)

    slopes = 2.0 ** (-8.0 * jnp.arange(1, DIFF_HEADS + 1, dtype=jnp.float32) / DIFF_HEADS)

    yd = _diff_call(slopes, diff_lambda[0], qdt, kd, vdt, kd_meta, vdt_meta, diff_subln_g[0].reshape(DIFF_V_DIM, 1))
    ys = _sb_call(qst, ks, vst, ks_meta, vst_meta, _neg_upper_ones(TS), _neg_upper_ones(N_META))

    out = _final_call(x.reshape(B * S, D), yd.reshape(B * S, D), ys.reshape(B * S, D), eg, eb,
                      w_all, w_br_diff[0].astype(bf16), w_br_sb[0].astype(bf16), w_out[0].astype(bf16),
                      b_gate[0], ln_g[0].reshape(1, D), ln_b[0].reshape(1, D))
    return out.reshape(B, S, D)
```

```python
import functools
import math

import jax
import jax.numpy as jnp
import numpy as np
from jax import lax
from jax.experimental import pallas as pl
from jax.experimental.pallas import tpu as pltpu

D_MODEL = 1024
N_META = 16
DIFF_HEADS = 8
DIFF_HEAD_DIM = 64
DIFF_V_DIM = 128
SB_HEADS = 16
SB_HEAD_DIM = 64
DEPTH = 1
DN_ALPHA = (2.0 * DEPTH) ** 0.25
LN_EPS = 1e-5
RMS_EPS = 1e-5
LAM_INIT = 0.8 - 0.6 * math.exp(-0.3 * 0)

COL_QD, COL_KD, COL_VD, COL_GD, COL_QS, COL_KS, COL_VS, COL_GS, COL_MD, COL_MS = range(10)
N_COLS = 10

LANES = 128
TD = 512
ONES_ROWS = 16
TS = 256
SB_HP = 4
TM_PROJ = 512
TM_FINAL = 512
FINAL_CHUNK = 256
VMEM_LIMIT = 56 * 1024 * 1024
LOG2E = 1.4426950408889634
NEG_BIG = -1e30
SB_EXIT_LOG = -105.0
EXP2_ZERO = 150.0
NORM_SLACK = 1.01
SCORE_MARGIN = 1.0

_NT = (((1,), (1,)), ((), ()))
_TN = (((0,), (1,)), ((), ()))


def _layer_norm(x, g, b):
    mu = jnp.mean(x, axis=-1, keepdims=True)
    xc = x - mu
    var = jnp.mean(xc * xc, axis=-1, keepdims=True)
    return xc * lax.rsqrt(var + LN_EPS) * g + b


def _proj_kernel(x_ref, g_ref, b_ref, wkd_ref, wks_ref, wqd_ref, wqs_ref, wvd_ref, wvs_ref,
                 kd_ref, ks_ref, qdt_ref, qst_ref, vdt_ref, vst_ref):
    h = _layer_norm(x_ref[...], g_ref[...], b_ref[...]).astype(jnp.bfloat16)
    for w_ref, o_ref in ((wkd_ref, kd_ref), (wks_ref, ks_ref)):
        o_ref[...] = jnp.dot(h, w_ref[...], preferred_element_type=jnp.float32).astype(jnp.bfloat16)
    for w_ref, o_ref, tk in ((wqd_ref, qdt_ref, TD), (wqs_ref, qst_ref, TS), (wvd_ref, vdt_ref, TD), (wvs_ref, vst_ref, TS)):
        xt = lax.dot_general(w_ref[...], h, _TN, preferred_element_type=jnp.float32).astype(jnp.bfloat16)
        for t in range(TM_PROJ // tk):
            o_ref[t] = xt[:, t * tk:(t + 1) * tk]


def _w_block(c, grid_rank):
    index_map = (lambda i: (0, c)) if grid_rank == 1 else (lambda bi, i: (0, c))
    return pl.BlockSpec((D_MODEL, D_MODEL), index_map, pipeline_mode=pl.Buffered(1))


def _proj_call(x, g, b, w_all):
    B, S, D = x.shape
    nt = S // TM_PROJ
    row_spec = pl.BlockSpec((None, TM_PROJ, D), lambda bi, i: (bi, i, 0))
    tiles = (TD, TS, TD, TS)
    t_specs = [pl.BlockSpec((None, TM_PROJ // tk, D, tk), lambda bi, i: (bi, i, 0, 0)) for tk in tiles]
    const = lambda bi, i: (0, 0)
    return pl.pallas_call(
        _proj_kernel,
        grid=(B, nt),
        in_specs=[row_spec,
                  pl.BlockSpec((1, D), const), pl.BlockSpec((1, D), const)]
        + [_w_block(c, 2) for c in (COL_KD, COL_KS, COL_QD, COL_QS, COL_VD, COL_VS)],
        out_specs=[row_spec, row_spec] + t_specs,
        out_shape=[jax.ShapeDtypeStruct((B, S, D), jnp.bfloat16)] * 2
        + [jax.ShapeDtypeStruct((B, S // tk, D, tk), jnp.bfloat16) for tk in tiles],
        compiler_params=pltpu.CompilerParams(dimension_semantics=("arbitrary", "arbitrary"),
                                             vmem_limit_bytes=VMEM_LIMIT),
        name="proj",
    )(x, g, b, *([w_all] * 6))


def _meta_proj_kernel(m_ref, g_ref, b_ref, wkd_ref, wks_ref, wvd_ref, wvs_ref, kd_ref, ks_ref, vdt_ref, vst_ref):
    h = _layer_norm(m_ref[...], g_ref[...], b_ref[...]).astype(jnp.bfloat16)
    for w_ref, o_ref in ((wkd_ref, kd_ref), (wks_ref, ks_ref)):
        o_ref[...] = jnp.dot(h, w_ref[...], preferred_element_type=jnp.float32).astype(jnp.bfloat16)
    for w_ref, o_ref in ((wvd_ref, vdt_ref), (wvs_ref, vst_ref)):
        o_ref[...] = lax.dot_general(w_ref[...], h, _TN, preferred_element_type=jnp.float32).astype(jnp.bfloat16)


def _meta_proj_call(meta, g, b, w_all):
    D = D_MODEL
    const = lambda i: (0, 0)
    full = lambda shape: pl.BlockSpec(shape, const)
    return pl.pallas_call(
        _meta_proj_kernel,
        grid=(1,),
        in_specs=[full((N_META, D)), full((1, D)), full((1, D))]
        + [_w_block(c, 1) for c in (COL_KD, COL_KS, COL_VD, COL_VS)],
        out_specs=[full((N_META, D)), full((N_META, D)), full((D, N_META)), full((D, N_META))],
        out_shape=[jax.ShapeDtypeStruct((N_META, D), jnp.bfloat16)] * 2
        + [jax.ShapeDtypeStruct((D, N_META), jnp.bfloat16)] * 2,
        compiler_params=pltpu.CompilerParams(dimension_semantics=("arbitrary",), vmem_limit_bytes=VMEM_LIMIT),
        name="meta_proj",
    )(meta, g, b, *([w_all] * 4))


def _diff_kernel(slope_ref, lam_ref, qt_ref, k_ref, vt_ref, km_ref, vmt_ref, g_ref, o_ref,
                 s_scr, p_scr, acc_scr, m_scr, bias_scr, qn_scr, kpre_ref):
    nq = qt_ref.shape[0]
    slope2 = slope_ref[pl.program_id(1)] * LOG2E
    f32, bf16 = jnp.float32, jnp.bfloat16

    row = lax.broadcasted_iota(jnp.int32, (TD, TD), 0)
    col = lax.broadcasted_iota(jnp.int32, (TD, TD), 1)
    bias = slope2 * row.astype(f32)
    bias_scr[0] = bias
    bias_scr[1] = jnp.where(row <= col, bias, NEG_BIG)

    def with_ones(vt):
        return jnp.concatenate([vt, jnp.ones((ONES_ROWS, vt.shape[1]), vt.dtype)], axis=0)

    def split(qt):
        zero = jnp.zeros_like(qt)
        rw = lax.broadcasted_iota(jnp.int32, qt.shape, 0)
        return jnp.where(rw < DIFF_HEAD_DIM, qt, zero), jnp.where(rw >= DIFF_HEAD_DIM, qt, zero)

    S = nq * TD
    qt_all = jnp.concatenate([qt_ref[t] for t in range(nq)], axis=1)
    colq = lax.broadcasted_iota(jnp.int32, (N_META, S), 1)
    rowm = lax.broadcasted_iota(jnp.int32, (N_META, S), 0)
    bias_m = slope2 * (rowm - N_META - (colq - (colq & (TD - 1)))).astype(f32)
    vmt_ones = with_ones(vmt_ref[...])
    qsq = jnp.square(qt_all.astype(f32))
    for c, qc in enumerate(split(qt_all)):
        s = jnp.dot(km_ref[...], qc, preferred_element_type=f32) + bias_m
        m0 = jnp.max(s, axis=0, keepdims=True)
        p = jnp.exp2(s - m0)
        a0 = jnp.dot(vmt_ones, p.astype(bf16), preferred_element_type=f32)
        qn = jnp.sqrt(jnp.sum(qsq[c * DIFF_HEAD_DIM:(c + 1) * DIFF_HEAD_DIM], axis=0, keepdims=True))
        for t in range(nq):
            m_scr[t, c] = m0[:, t * TD:(t + 1) * TD]
            acc_scr[t, c] = a0[:, t * TD:(t + 1) * TD]
            qn_scr[t, c] = qn[:, t * TD:(t + 1) * TD]

    k_all = k_ref[...]
    ksq = k_all * k_all
    ln8 = lax.broadcasted_iota(jnp.int32, (8, LANES), 1)
    for c, half in enumerate((ln8 < DIFF_HEAD_DIM, ln8 >= DIFF_HEAD_DIM)):
        kn = jnp.sqrt(lax.dot_general(jnp.where(half, 1.0, 0.0).astype(bf16), ksq, _NT,
                                      preferred_element_type=f32)[0:1, :])
        kmax = jnp.float32(0.0)
        for t in range(nq):
            kmax = jnp.maximum(kmax, jnp.max(kn[:, t * TD:(t + 1) * TD]))
            kpre_ref[c * nq + t] = kmax

    def stage_a(qt, kt):
        k_tile = k_ref[pl.ds(pl.multiple_of(kt * TD, TD), TD), :]
        qcs = split(qt_ref[qt])
        tile_bias = bias_scr[(kt == qt).astype(jnp.int32)]
        mx = []
        for c in range(2):
            s = jnp.dot(k_tile, qcs[c], preferred_element_type=f32) + tile_bias
            s_scr[c] = s
            mx.append(jnp.max(s, axis=0, keepdims=True))
        return tuple(mx)

    def stage_b(qt, kt, mx):
        off2 = -(slope2 * TD) * (qt - kt).astype(f32)
        alphas, m_news = [], []
        for c in range(2):
            m_old = m_scr[qt, c]
            m_new = jnp.maximum(m_old, mx[c] + off2)
            alpha = jnp.exp2(m_old - m_new)
            p = jnp.exp2(s_scr[c] - (m_new - off2))
            m_scr[qt, c] = m_new
            p_scr[c] = p.astype(bf16)
            alphas.append(alpha)
            m_news.append(m_new)
        return tuple(alphas), tuple(m_news)

    def rest_is_zero(qt, kt, m_news):
        worst_bias = slope2 * (TD - 1) - (slope2 * TD) * (qt - kt).astype(f32)
        slack = None
        for c in range(2):
            bound = qn_scr[qt, c] * (kpre_ref[c * nq + kt] * NORM_SLACK) + (worst_bias + SCORE_MARGIN)
            gap = (m_news[c] - EXP2_ZERO) - bound
            slack = gap if slack is None else jnp.minimum(slack, gap)
        return jnp.min(slack) > 0.0

    def stage_c(qt, kt, alphas):
        vt_tile = with_ones(vt_ref[kt])
        for c in range(2):
            acc_scr[qt, c] = alphas[c] * acc_scr[qt, c] + jnp.dot(vt_tile, p_scr[c], preferred_element_type=f32)

    lam = (jnp.exp(jnp.sum(lam_ref[0:1, :] * lam_ref[1:2, :], axis=1, keepdims=True))
           - jnp.exp(jnp.sum(lam_ref[2:3, :] * lam_ref[3:4, :], axis=1, keepdims=True)) + LAM_INIT)

    def finalize(qt):
        num = [acc_scr[qt, c, 0:DIFF_V_DIM, :] for c in range(2)]
        den = [acc_scr[qt, c, DIFF_V_DIM:DIFF_V_DIM + 1, :] for c in range(2)]
        y = num[0] * (1.0 / den[0]) - lam * (num[1] * (1.0 / den[1]))
        ms = jnp.mean(y * y, axis=0, keepdims=True)
        y = y * lax.rsqrt(ms + RMS_EPS) * g_ref[...] * (1.0 - LAM_INIT)
        o_ref[pl.ds(pl.multiple_of(qt * TD, TD), TD), :] = y.T.astype(o_ref.dtype)

    p_scr[...] = jnp.zeros(p_scr.shape, bf16)
    ones = jnp.ones((1, TD), f32)
    zero_i = jnp.int32(0)
    mx0 = stage_a(zero_i, zero_i)

    def cond(st):
        return st[2] < nq

    def body(st):
        qa, ka, qb, kb, eb, qc, kc, ec, mx, alphas = st
        stage_c(qc, kc, alphas)
        alphas, m_news = stage_b(qb, kb, mx)
        qa_c, ka_c = jnp.minimum(qa, nq - 1), jnp.minimum(ka, nq - 1)
        ea = jnp.logical_or(ka == 0, jnp.logical_and(qa == qb, rest_is_zero(qa_c, jnp.maximum(ka_c - 1, 0), m_news)))
        mx = stage_a(qa_c, ka_c)

        @pl.when(ec)
        def _():
            finalize(qc)

        return (jnp.where(ea, qa + 1, qa), jnp.where(ea, qa + 1, ka - 1), qa, ka, ea, qb, kb, eb, mx, alphas)

    st = (jnp.int32(1), jnp.int32(1), zero_i, zero_i, jnp.bool_(True), zero_i, zero_i, jnp.bool_(False),
          mx0, (ones, ones))
    st = lax.while_loop(cond, body, st)
    _, _, _, _, _, qc, kc, _, _, alphas = st
    stage_c(qc, kc, alphas)
    finalize(qc)


def _diff_call(slopes, lam, qdt, kd, vdt, kdm, vdmt, g_col):
    B, S, D = kd.shape
    nq = S // TD
    smem = pl.BlockSpec(memory_space=pltpu.SMEM)
    slab = pl.BlockSpec((None, S, LANES), lambda b, h: (b, 0, h))
    tiles_t = pl.BlockSpec((None, nq, LANES, TD), lambda b, h: (b, 0, h, 0))
    return pl.pallas_call(
        _diff_kernel,
        grid=(B, DIFF_HEADS),
        in_specs=[smem, pl.BlockSpec((4, DIFF_HEAD_DIM), lambda b, h: (0, 0)),
                  tiles_t, slab, tiles_t,
                  pl.BlockSpec((N_META, LANES), lambda b, h: (0, h)),
                  pl.BlockSpec((DIFF_V_DIM, N_META), lambda b, h: (h, 0)),
                  pl.BlockSpec((DIFF_V_DIM, 1), lambda b, h: (0, 0))],
        out_specs=slab,
        out_shape=jax.ShapeDtypeStruct((B, S, D), jnp.bfloat16),
        scratch_shapes=[pltpu.VMEM((2, TD, TD), jnp.float32),
                        pltpu.VMEM((2, TD, TD), jnp.bfloat16),
                        pltpu.VMEM((nq, 2, DIFF_V_DIM + ONES_ROWS, TD), jnp.float32),
                        pltpu.VMEM((nq, 2, 1, TD), jnp.float32),
                        pltpu.VMEM((2, TD, TD), jnp.float32),
                        pltpu.VMEM((nq, 2, 1, TD), jnp.float32),
                        pltpu.SMEM((2 * nq,), jnp.float32)],
        compiler_params=pltpu.CompilerParams(dimension_semantics=("arbitrary",) * 2,
                                             vmem_limit_bytes=VMEM_LIMIT),
        name="diff_attn",
    )(slopes, lam, qdt, kd, vdt, kdm, vdmt, g_col)


def _sb_kernel(qt_ref, k_ref, vt_ref, km_ref, vmt_ref, u_ref, um_ref, o_ref, zi_scr, out_scr):
    nq = qt_ref.shape[0]
    f32, bf16 = jnp.float32, jnp.bfloat16
    u = u_ref[...]
    row = lax.broadcasted_iota(jnp.int32, (TS, TS), 0)
    col = lax.broadcasted_iota(jnp.int32, (TS, TS), 1)
    strict = row < col

    def rows(i):
        return pl.ds(pl.multiple_of(i * TS, TS), TS)

    def lane_tile(hh):
        lt = hh * SB_HEAD_DIM // LANES
        return slice(lt * LANES, (lt + 1) * LANES)

    def head_queries(i):
        rw = lax.broadcasted_iota(jnp.int32, (LANES, TS), 0)
        out = []
        for hh in range(SB_HP):
            slab = qt_ref[i, lane_tile(hh), :]
            lo = hh * SB_HEAD_DIM % LANES
            out.append(jnp.where(jnp.logical_and(rw >= lo, rw < lo + SB_HEAD_DIM), slab, jnp.zeros_like(slab)))
        return out

    def head_rows(hh):
        return slice(hh * SB_HEAD_DIM, (hh + 1) * SB_HEAD_DIM)

    def suffix_logits(k_tile, qh, mask, umat):
        z = jnp.dot(k_tile, qh, preferred_element_type=f32)
        e = jnp.exp2(jnp.abs(z) * (-LOG2E))
        sp = jnp.maximum(z, 0.0) + jnp.log(1.0 + e)
        if mask is not None:
            sp = jnp.where(mask, sp, 0.0)
        incl = jnp.dot(umat, sp.astype(bf16), preferred_element_type=f32)
        return z + incl, incl[0:1, :]

    def stage_a(i):
        qhs = head_queries(i)
        il = jnp.maximum(i - 1, 0)
        k_d, k_l = k_ref[rows(i), :], k_ref[rows(il), :]
        tots = []
        for ch in range(2 * SB_HP):
            diag, hh = ch < SB_HP, ch % SB_HP
            zi, tot = suffix_logits((k_d if diag else k_l)[:, lane_tile(hh)], qhs[hh], strict if diag else None, u)
            zi_scr[ch] = zi
            tots.append(tot)
        return tuple(tots)

    def stage_b(i, tots):
        has_left = i > 0
        il = jnp.maximum(i - 1, 0)
        carries = []
        for hh in range(SB_HP):
            w_d = jnp.where(strict, jnp.exp(zi_scr[hh]), 0.0).astype(bf16)
            w_l = jnp.exp(zi_scr[SB_HP + hh] + jnp.where(has_left, tots[hh], NEG_BIG)).astype(bf16)
            out_scr[head_rows(hh), :] = (jnp.dot(vt_ref[i, head_rows(hh), :], w_d, preferred_element_type=f32)
                                         + jnp.dot(vt_ref[il, head_rows(hh), :], w_l, preferred_element_type=f32))
            carries.append(tots[hh] + jnp.where(has_left, tots[SB_HP + hh], 0.0))
        return carries

    def further_tiles(i, carries):
        qhs = head_queries(i)
        for hh in range(SB_HP):

            def tile(k_tile, v_tile, umat, carry, acc):
                zi, tot = suffix_logits(k_tile, qhs[hh], None, umat)
                w = jnp.exp(zi + carry).astype(bf16)
                return carry + tot, acc + jnp.dot(v_tile, w, preferred_element_type=f32)

            def cond(st):
                j, cmax, _, _ = st
                return jnp.logical_and(j >= 0, cmax > SB_EXIT_LOG)

            def body(st):
                j, _, carry, acc = st
                carry, acc = tile(k_ref[rows(j), lane_tile(hh)], vt_ref[j, head_rows(hh), :], u, carry, acc)
                return j - 1, jnp.max(carry), carry, acc

            st = (i - 2, jnp.max(carries[hh]), carries[hh], out_scr[head_rows(hh), :])
            _, cmax, carry, acc = lax.while_loop(cond, body, st)

            def meta(acc):
                return tile(km_ref[:, lane_tile(hh)], vmt_ref[head_rows(hh), :], um_ref[...], carry, acc)[1]

            out_scr[head_rows(hh), :] = lax.cond(cmax > SB_EXIT_LOG, meta, lambda a: a, acc)

    def write_out(i):
        o_ref[rows(i), :] = out_scr[...].T.astype(o_ref.dtype)

    zi_scr[...] = jnp.zeros(zi_scr.shape, f32)
    out_scr[...] = jnp.zeros(out_scr.shape, f32)
    zero_row = jnp.zeros((1, TS), f32)

    def body(i, tots):
        ib = jnp.maximum(i - 1, 0)
        write_out(jnp.maximum(i - 2, 0))
        carries = stage_b(ib, tots)
        tots = stage_a(jnp.minimum(i, nq - 1))
        worst = functools.reduce(jnp.maximum, carries)

        @pl.when(jnp.logical_and(i >= 1, jnp.max(worst) > SB_EXIT_LOG))
        def _():
            further_tiles(ib, carries)

        return tots

    lax.fori_loop(0, nq + 1, body, (zero_row,) * (2 * SB_HP))
    write_out(nq - 1)


def _sb_call(qst, ks, vst, ksm, vsmt, u, um):
    B, S, D = ks.shape
    width = SB_HP * SB_HEAD_DIM
    const = lambda b, h: (0, 0)
    slab = pl.BlockSpec((None, S, width), lambda b, h: (b, 0, h))
    tiles_t = pl.BlockSpec((None, S // TS, width, TS), lambda b, h: (b, 0, h, 0))
    return pl.pallas_call(
        _sb_kernel,
        grid=(B, SB_HEADS // SB_HP),
        in_specs=[tiles_t, slab, tiles_t,
                  pl.BlockSpec((N_META, width), lambda b, h: (0, h)),
                  pl.BlockSpec((width, N_META), lambda b, h: (h, 0)),
                  pl.BlockSpec((TS, TS), const), pl.BlockSpec((N_META, N_META), const)],
        out_specs=slab,
        out_shape=jax.ShapeDtypeStruct((B, S, D), jnp.bfloat16),
        scratch_shapes=[pltpu.VMEM((2 * SB_HP, TS, TS), jnp.float32),
                        pltpu.VMEM((width, TS), jnp.float32)],
        compiler_params=pltpu.CompilerParams(dimension_semantics=("arbitrary",) * 2,
                                             vmem_limit_bytes=VMEM_LIMIT),
        name="sb_attn",
    )(qst, ks, vst, ksm, vsmt, u, um)


def _sigmoid(x):
    return 1.0 / (1.0 + jnp.exp(-x))


def _final_kernel(x_ref, yd_ref, ys_ref, eg_ref, eb_ref, wgd_ref, wgs_ref, wmd_ref, wms_ref,
                  wbd_ref, wbs_ref, wo_ref, bg_ref, lg_ref, lb_ref, o_ref):
    for r in range(TM_FINAL // FINAL_CHUNK):
        rs = slice(r * FINAL_CHUNK, (r + 1) * FINAL_CHUNK)
        h = _layer_norm(x_ref[rs, :], eg_ref[...], eb_ref[...])
        hb = h.astype(jnp.bfloat16)

        def proj(w_ref):
            return jnp.dot(hb, w_ref[...], preferred_element_type=jnp.float32)

        g_d = proj(wgd_ref)
        u_d = (yd_ref[rs, :].astype(jnp.float32) * (g_d * _sigmoid(g_d))).astype(jnp.bfloat16)
        br_d = jnp.dot(u_d, wbd_ref[...], preferred_element_type=jnp.float32)
        merged = _sigmoid(proj(wmd_ref) + bg_ref[0:1, :]) * br_d
        g_s = proj(wgs_ref)
        u_s = (ys_ref[rs, :].astype(jnp.float32) * (g_s * _sigmoid(g_s))).astype(jnp.bfloat16)
        br_s = jnp.dot(u_s, wbs_ref[...], preferred_element_type=jnp.float32)
        merged = merged + _sigmoid(proj(wms_ref) + bg_ref[1:2, :]) * br_s
        y = jnp.dot(merged.astype(jnp.bfloat16), wo_ref[...], preferred_element_type=jnp.float32)
        o_ref[rs, :] = _layer_norm(DN_ALPHA * h + y, lg_ref[...], lb_ref[...]).astype(o_ref.dtype)


def _final_call(x2, yd2, ys2, eg, eb, w_all, wbd, wbs, wo, bg, lg, lb):
    R, D = x2.shape
    row = pl.BlockSpec((TM_FINAL, D), lambda i: (i, 0))
    const = lambda i: (0, 0)
    vec = pl.BlockSpec((1, D), const)
    weight = pl.BlockSpec((D, D), const, pipeline_mode=pl.Buffered(1))
    return pl.pallas_call(
        _final_kernel,
        grid=(R // TM_FINAL,),
        in_specs=[row, row, row, vec, vec]
        + [_w_block(c, 1) for c in (COL_GD, COL_GS, COL_MD, COL_MS)]
        + [weight, weight, weight, pl.BlockSpec((2, D), const), vec, vec],
        out_specs=row,
        out_shape=jax.ShapeDtypeStruct((R, D), jnp.float32),
        compiler_params=pltpu.CompilerParams(dimension_semantics=("arbitrary",),
                                             vmem_limit_bytes=VMEM_LIMIT),
        name="final",
    )(x2, yd2, ys2, eg, eb, w_all, w_all, w_all, w_all, wbd, wbs, wo, bg, lg, lb)


def _neg_upper_ones(n):
    r = jnp.arange(n)
    return jnp.where(r[None, :] >= r[:, None], -1.0, 0.0).astype(jnp.bfloat16)


def kernel(x, meta_tokens, emb_ln_g, emb_ln_b, w_in, b_gate, diff_lambda, diff_subln_g,
           w_br_diff, w_br_sb, w_out, ln_g, ln_b):
    B, S, D = x.shape
    bf16 = jnp.bfloat16
    qscale = DIFF_HEAD_DIM ** -0.5
    col_scale = np.ones((N_COLS * D,), np.float32)
    col_scale[COL_QD * D:(COL_QD + 1) * D] = qscale * LOG2E
    col_scale[COL_QS * D:(COL_QS + 1) * D] = qscale
    w_all = (w_in[0] * col_scale).astype(bf16)
    eg, eb = emb_ln_g.reshape(1, D), emb_ln_b.reshape(1, D)

    kd, ks, qdt, qst, vdt, vst = _proj_call(x, eg, eb, w_all)
    kd_meta, ks_meta, vdt_meta, vst_meta = _meta_proj_call(meta_tokens, eg, eb, w_all)

    slopes = 2.0 ** (-8.0 * jnp.arange(1, DIFF_HEADS + 1, dtype=jnp.float32) / DIFF_HEADS)

    yd = _diff_call(slopes, diff_lambda[0], qdt, kd, vdt, kd_meta, vdt_meta, diff_subln_g[0].reshape(DIFF_V_DIM, 1))
    ys = _sb_call(qst, ks, vst, ks_meta, vst_meta, _neg_upper_ones(TS), _neg_upper_ones(N_META))

    out = _final_call(x.reshape(B * S, D), yd.reshape(B * S, D), ys.reshape(B * S, D), eg, eb,
                      w_all, w_br_diff[0].astype(bf16), w_br_sb[0].astype(bf16), w_out[0].astype(bf16),
                      b_gate[0], ln_g[0].reshape(1, D), ln_b[0].reshape(1, D))
    return out.reshape(B, S, D)
```

```python
import functools
import math

import jax
import jax.numpy as jnp
import numpy as np
from jax import lax
from jax.experimental import pallas as pl
from jax.experimental.pallas import tpu as pltpu

D_MODEL = 1024
N_META = 16
DIFF_HEADS = 8
DIFF_HEAD_DIM = 64
DIFF_V_DIM = 128
SB_HEADS = 16
SB_HEAD_DIM = 64
DEPTH = 1
DN_ALPHA = (2.0 * DEPTH) ** 0.25
LN_EPS = 1e-5
RMS_EPS = 1e-5
LAM_INIT = 0.8 - 0.6 * math.exp(-0.3 * 0)

COL_QD, COL_KD, COL_VD, COL_GD, COL_QS, COL_KS, COL_VS, COL_GS, COL_MD, COL_MS = range(10)
N_COLS = 10

LANES = 128
TD = 512
ONES_ROWS = 16
TS = 256
SB_HP = 4
TM_PROJ = 512
TM_FINAL = 512
FINAL_CHUNK = 256
VMEM_LIMIT = 56 * 1024 * 1024
LOG2E = 1.4426950408889634
NEG_BIG = -1e30
SB_EXIT_LOG = -105.0
EXP2_ZERO = 150.0
NORM_SLACK = 1.01
SCORE_MARGIN = 1.0

_NT = (((1,), (1,)), ((), ()))
_TN = (((0,), (1,)), ((), ()))


def _layer_norm(x, g, b):
    mu = jnp.mean(x, axis=-1, keepdims=True)
    xc = x - mu
    var = jnp.mean(xc * xc, axis=-1, keepdims=True)
    return xc * lax.rsqrt(var + LN_EPS) * g + b


def _proj_kernel(x_ref, g_ref, b_ref, wkd_ref, wks_ref, wqd_ref, wqs_ref, wvd_ref, wvs_ref, gsel_ref,
                 kd_ref, ks_ref, qdt_ref, qst_ref, vdt_ref, vst_ref, kn2_ref):
    h = _layer_norm(x_ref[...], g_ref[...], b_ref[...]).astype(jnp.bfloat16)
    kd = jnp.dot(h, wkd_ref[...], preferred_element_type=jnp.float32)
    kd_ref[...] = kd.astype(jnp.bfloat16)
    norm2 = jnp.dot((kd * kd).astype(jnp.bfloat16), gsel_ref[...], preferred_element_type=jnp.float32)
    kn2_ref[...] = jnp.broadcast_to(jnp.max(norm2, axis=0, keepdims=True), kn2_ref.shape)
    ks_ref[...] = jnp.dot(h, wks_ref[...], preferred_element_type=jnp.float32).astype(jnp.bfloat16)
    for w_ref, o_ref, tk in ((wqd_ref, qdt_ref, TD), (wqs_ref, qst_ref, TS), (wvd_ref, vdt_ref, TD), (wvs_ref, vst_ref, TS)):
        xt = lax.dot_general(w_ref[...], h, _TN, preferred_element_type=jnp.float32).astype(jnp.bfloat16)
        for t in range(TM_PROJ // tk):
            o_ref[t] = xt[:, t * tk:(t + 1) * tk]


def _w_block(c, grid_rank):
    index_map = (lambda i: (0, c)) if grid_rank == 1 else (lambda bi, i: (0, c))
    return pl.BlockSpec((D_MODEL, D_MODEL), index_map, pipeline_mode=pl.Buffered(1))


def _proj_call(x, g, b, w_all, gsel):
    B, S, D = x.shape
    assert TM_PROJ == TD, "the projection step must cover exactly one differential key tile"
    nt = S // TM_PROJ
    row_spec = pl.BlockSpec((None, TM_PROJ, D), lambda bi, i: (bi, i, 0))
    tiles = (TD, TS, TD, TS)
    t_specs = [pl.BlockSpec((None, TM_PROJ // tk, D, tk), lambda bi, i: (bi, i, 0, 0)) for tk in tiles]
    const = lambda bi, i: (0, 0)
    return pl.pallas_call(
        _proj_kernel,
        grid=(B, nt),
        in_specs=[row_spec,
                  pl.BlockSpec((1, D), const), pl.BlockSpec((1, D), const)]
        + [_w_block(c, 2) for c in (COL_KD, COL_KS, COL_QD, COL_QS, COL_VD, COL_VS)]
        + [pl.BlockSpec((D, LANES), const, pipeline_mode=pl.Buffered(1))],
        out_specs=[row_spec, row_spec] + t_specs + [pl.BlockSpec((None, None, 8, LANES), lambda bi, i: (bi, i, 0, 0))],
        out_shape=[jax.ShapeDtypeStruct((B, S, D), jnp.bfloat16)] * 2
        + [jax.ShapeDtypeStruct((B, S // tk, D, tk), jnp.bfloat16) for tk in tiles]
        + [jax.ShapeDtypeStruct((B, nt, 8, LANES), jnp.float32)],
        compiler_params=pltpu.CompilerParams(dimension_semantics=("arbitrary", "arbitrary"),
                                             vmem_limit_bytes=VMEM_LIMIT),
        name="proj",
    )(x, g, b, *([w_all] * 6), gsel)


def _meta_proj_kernel(m_ref, g_ref, b_ref, wkd_ref, wks_ref, wvd_ref, wvs_ref, kd_ref, ks_ref, vdt_ref, vst_ref):
    h = _layer_norm(m_ref[...], g_ref[...], b_ref[...]).astype(jnp.bfloat16)
    for w_ref, o_ref in ((wkd_ref, kd_ref), (wks_ref, ks_ref)):
        o_ref[...] = jnp.dot(h, w_ref[...], preferred_element_type=jnp.float32).astype(jnp.bfloat16)
    for w_ref, o_ref in ((wvd_ref, vdt_ref), (wvs_ref, vst_ref)):
        o_ref[...] = lax.dot_general(w_ref[...], h, _TN, preferred_element_type=jnp.float32).astype(jnp.bfloat16)


def _meta_proj_call(meta, g, b, w_all):
    D = D_MODEL
    const = lambda i: (0, 0)
    full = lambda shape: pl.BlockSpec(shape, const)
    return pl.pallas_call(
        _meta_proj_kernel,
        grid=(1,),
        in_specs=[full((N_META, D)), full((1, D)), full((1, D))]
        + [_w_block(c, 1) for c in (COL_KD, COL_KS, COL_VD, COL_VS)],
        out_specs=[full((N_META, D)), full((N_META, D)), full((D, N_META)), full((D, N_META))],
        out_shape=[jax.ShapeDtypeStruct((N_META, D), jnp.bfloat16)] * 2
        + [jax.ShapeDtypeStruct((D, N_META), jnp.bfloat16)] * 2,
        compiler_params=pltpu.CompilerParams(dimension_semantics=("arbitrary",), vmem_limit_bytes=VMEM_LIMIT),
        name="meta_proj",
    )(meta, g, b, *([w_all] * 4))


def _diff_kernel(slope_ref, kpre_ref, lam_ref, qt_ref, k_ref, vt_ref, km_ref, vmt_ref, g_ref, o_ref,
                 s_scr, p_scr, acc_scr, m_scr, bias_scr, qn_scr):
    nq = qt_ref.shape[0]
    slope2 = slope_ref[pl.program_id(1)] * LOG2E
    f32, bf16 = jnp.float32, jnp.bfloat16

    row = lax.broadcasted_iota(jnp.int32, (TD, TD), 0)
    col = lax.broadcasted_iota(jnp.int32, (TD, TD), 1)
    bias = slope2 * row.astype(f32)
    bias_scr[0] = bias
    bias_scr[1] = jnp.where(row <= col, bias, NEG_BIG)

    def with_ones(vt):
        return jnp.concatenate([vt, jnp.ones((ONES_ROWS, vt.shape[1]), vt.dtype)], axis=0)

    def split(qt):
        zero = jnp.zeros_like(qt)
        rw = lax.broadcasted_iota(jnp.int32, qt.shape, 0)
        return jnp.where(rw < DIFF_HEAD_DIM, qt, zero), jnp.where(rw >= DIFF_HEAD_DIM, qt, zero)

    S = nq * TD
    qt_all = jnp.concatenate([qt_ref[t] for t in range(nq)], axis=1)
    colq = lax.broadcasted_iota(jnp.int32, (N_META, S), 1)
    rowm = lax.broadcasted_iota(jnp.int32, (N_META, S), 0)
    bias_m = slope2 * (rowm - N_META - (colq - (colq & (TD - 1)))).astype(f32)
    vmt_ones = with_ones(vmt_ref[...])
    qsq = jnp.square(qt_all.astype(f32))
    for c, qc in enumerate(split(qt_all)):
        s = jnp.dot(km_ref[...], qc, preferred_element_type=f32) + bias_m
        m0 = jnp.max(s, axis=0, keepdims=True)
        p = jnp.exp2(s - m0)
        a0 = jnp.dot(vmt_ones, p.astype(bf16), preferred_element_type=f32)
        qn = jnp.sqrt(jnp.sum(qsq[c * DIFF_HEAD_DIM:(c + 1) * DIFF_HEAD_DIM], axis=0, keepdims=True))
        for t in range(nq):
            m_scr[t, c] = m0[:, t * TD:(t + 1) * TD]
            acc_scr[t, c] = a0[:, t * TD:(t + 1) * TD]
            qn_scr[t, c] = qn[:, t * TD:(t + 1) * TD]

    kpre_base = (pl.program_id(0) * DIFF_HEADS + pl.program_id(1)) * (2 * nq)

    def stage_a(qt, kt):
        k_tile = k_ref[pl.ds(pl.multiple_of(kt * TD, TD), TD), :]
        qcs = split(qt_ref[qt])
        tile_bias = bias_scr[(kt == qt).astype(jnp.int32)]
        mx = []
        for c in range(2):
            s = jnp.dot(k_tile, qcs[c], preferred_element_type=f32) + tile_bias
            s_scr[c] = s
            mx.append(jnp.max(s, axis=0, keepdims=True))
        return tuple(mx)

    def stage_b(qt, kt, mx):
        off2 = -(slope2 * TD) * (qt - kt).astype(f32)
        alphas, m_news = [], []
        for c in range(2):
            m_old = m_scr[qt, c]
            m_new = jnp.maximum(m_old, mx[c] + off2)
            alpha = jnp.exp2(m_old - m_new)
            p = jnp.exp2(s_scr[c] - (m_new - off2))
            m_scr[qt, c] = m_new
            p_scr[c] = p.astype(bf16)
            alphas.append(alpha)
            m_news.append(m_new)
        return tuple(alphas), tuple(m_news)

    def rest_is_zero(qt, kt, m_news):
        worst_bias = slope2 * (TD - 1) - (slope2 * TD) * (qt - kt).astype(f32)
        slack = None
        for c in range(2):
            bound = qn_scr[qt, c] * (kpre_ref[kpre_base + c * nq + kt] * NORM_SLACK) + (worst_bias + SCORE_MARGIN)
            gap = (m_news[c] - EXP2_ZERO) - bound
            slack = gap if slack is None else jnp.minimum(slack, gap)
        return jnp.min(slack) > 0.0

    def stage_c(qt, kt, alphas):
        vt_tile = with_ones(vt_ref[kt])
        for c in range(2):
            acc_scr[qt, c] = alphas[c] * acc_scr[qt, c] + jnp.dot(vt_tile, p_scr[c], preferred_element_type=f32)

    lam = (jnp.exp(jnp.sum(lam_ref[0:1, :] * lam_ref[1:2, :], axis=1, keepdims=True))
           - jnp.exp(jnp.sum(lam_ref[2:3, :] * lam_ref[3:4, :], axis=1, keepdims=True)) + LAM_INIT)

    def finalize(qt):
        num = [acc_scr[qt, c, 0:DIFF_V_DIM, :] for c in range(2)]
        den = [acc_scr[qt, c, DIFF_V_DIM:DIFF_V_DIM + 1, :] for c in range(2)]
        y = num[0] * (1.0 / den[0]) - lam * (num[1] * (1.0 / den[1]))
        ms = jnp.mean(y * y, axis=0, keepdims=True)
        y = y * lax.rsqrt(ms + RMS_EPS) * g_ref[...] * (1.0 - LAM_INIT)
        o_ref[pl.ds(pl.multiple_of(qt * TD, TD), TD), :] = y.T.astype(o_ref.dtype)

    p_scr[...] = jnp.zeros(p_scr.shape, bf16)
    ones = jnp.ones((1, TD), f32)
    zero_i = jnp.int32(0)
    mx0 = stage_a(zero_i, zero_i)

    def cond(st):
        return st[2] < nq

    def body(st):
        qa, ka, qb, kb, eb, qc, kc, ec, mx, alphas = st
        stage_c(qc, kc, alphas)
        alphas, m_news = stage_b(qb, kb, mx)
        qa_c, ka_c = jnp.minimum(qa, nq - 1), jnp.minimum(ka, nq - 1)
        ea = jnp.logical_or(ka == 0, jnp.logical_and(qa == qb, rest_is_zero(qa_c, jnp.maximum(ka_c - 1, 0), m_news)))
        mx = stage_a(qa_c, ka_c)

        @pl.when(ec)
        def _():
            finalize(qc)

        return (jnp.where(ea, qa + 1, qa), jnp.where(ea, qa + 1, ka - 1), qa, ka, ea, qb, kb, eb, mx, alphas)

    st = (jnp.int32(1), jnp.int32(1), zero_i, zero_i, jnp.bool_(True), zero_i, zero_i, jnp.bool_(False),
          mx0, (ones, ones))
    st = lax.while_loop(cond, body, st)
    _, _, _, _, _, qc, kc, _, _, alphas = st
    stage_c(qc, kc, alphas)
    finalize(qc)


def _diff_call(slopes, kpre, lam, qdt, kd, vdt, kdm, vdmt, g_col):
    B, S, D = kd.shape
    nq = S // TD
    smem = pl.BlockSpec(memory_space=pltpu.SMEM)
    slab = pl.BlockSpec((None, S, LANES), lambda b, h: (b, 0, h))
    tiles_t = pl.BlockSpec((None, nq, LANES, TD), lambda b, h: (b, 0, h, 0))
    return pl.pallas_call(
        _diff_kernel,
        grid=(B, DIFF_HEADS),
        in_specs=[smem, smem, pl.BlockSpec((4, DIFF_HEAD_DIM), lambda b, h: (0, 0)),
                  tiles_t, slab, tiles_t,
                  pl.BlockSpec((N_META, LANES), lambda b, h: (0, h)),
                  pl.BlockSpec((DIFF_V_DIM, N_META), lambda b, h: (h, 0)),
                  pl.BlockSpec((DIFF_V_DIM, 1), lambda b, h: (0, 0))],
        out_specs=slab,
        out_shape=jax.ShapeDtypeStruct((B, S, D), jnp.bfloat16),
        scratch_shapes=[pltpu.VMEM((2, TD, TD), jnp.float32),
                        pltpu.VMEM((2, TD, TD), jnp.bfloat16),
                        pltpu.VMEM((nq, 2, DIFF_V_DIM + ONES_ROWS, TD), jnp.float32),
                        pltpu.VMEM((nq, 2, 1, TD), jnp.float32),
                        pltpu.VMEM((2, TD, TD), jnp.float32),
                        pltpu.VMEM((nq, 2, 1, TD), jnp.float32)],
        compiler_params=pltpu.CompilerParams(dimension_semantics=("arbitrary",) * 2,
                                             vmem_limit_bytes=VMEM_LIMIT),
        name="diff_attn",
    )(slopes, kpre, lam, qdt, kd, vdt, kdm, vdmt, g_col)


def _sb_kernel(qt_ref, k_ref, vt_ref, km_ref, vmt_ref, u_ref, um_ref, o_ref, zi_scr, out_scr):
    nq = qt_ref.shape[0]
    f32, bf16 = jnp.float32, jnp.bfloat16
    u = u_ref[...]
    row = lax.broadcasted_iota(jnp.int32, (TS, TS), 0)
    col = lax.broadcasted_iota(jnp.int32, (TS, TS), 1)
    strict = row < col

    def rows(i):
        return pl.ds(pl.multiple_of(i * TS, TS), TS)

    def lane_tile(hh):
        lt = hh * SB_HEAD_DIM // LANES
        return slice(lt * LANES, (lt + 1) * LANES)

    def head_queries(i):
        rw = lax.broadcasted_iota(jnp.int32, (LANES, TS), 0)
        out = []
        for hh in range(SB_HP):
            slab = qt_ref[i, lane_tile(hh), :]
            lo = hh * SB_HEAD_DIM % LANES
            out.append(jnp.where(jnp.logical_and(rw >= lo, rw < lo + SB_HEAD_DIM), slab, jnp.zeros_like(slab)))
        return out

    def head_rows(hh):
        return slice(hh * SB_HEAD_DIM, (hh + 1) * SB_HEAD_DIM)

    def suffix_logits(k_tile, qh, mask, umat):
        z = jnp.dot(k_tile, qh, preferred_element_type=f32)
        e = jnp.exp2(jnp.abs(z) * (-LOG2E))
        sp = jnp.maximum(z, 0.0) + jnp.log(1.0 + e)
        if mask is not None:
            sp = jnp.where(mask, sp, 0.0)
        incl = jnp.dot(umat, sp.astype(bf16), preferred_element_type=f32)
        return z + incl, incl[0:1, :]

    def stage_a(i):
        qhs = head_queries(i)
        il = jnp.maximum(i - 1, 0)
        k_d, k_l = k_ref[rows(i), :], k_ref[rows(il), :]
        tots = []
        for ch in range(2 * SB_HP):
            diag, hh = ch < SB_HP, ch % SB_HP
            zi, tot = suffix_logits((k_d if diag else k_l)[:, lane_tile(hh)], qhs[hh], strict if diag else None, u)
            zi_scr[ch] = zi
            tots.append(tot)
        return tuple(tots)

    def stage_b(i, tots):
        has_left = i > 0
        il = jnp.maximum(i - 1, 0)
        carries = []
        for hh in range(SB_HP):
            w_d = jnp.where(strict, jnp.exp(zi_scr[hh]), 0.0).astype(bf16)
            w_l = jnp.exp(zi_scr[SB_HP + hh] + jnp.where(has_left, tots[hh], NEG_BIG)).astype(bf16)
            out_scr[head_rows(hh), :] = (jnp.dot(vt_ref[i, head_rows(hh), :], w_d, preferred_element_type=f32)
                                         + jnp.dot(vt_ref[il, head_rows(hh), :], w_l, preferred_element_type=f32))
            carries.append(tots[hh] + jnp.where(has_left, tots[SB_HP + hh], 0.0))
        return carries

    def further_tiles(i, carries):
        qhs = head_queries(i)
        for hh in range(SB_HP):

            def tile(k_tile, v_tile, umat, carry, acc):
                zi, tot = suffix_logits(k_tile, qhs[hh], None, umat)
                w = jnp.exp(zi + carry).astype(bf16)
                return carry + tot, acc + jnp.dot(v_tile, w, preferred_element_type=f32)

            def cond(st):
                j, cmax, _, _ = st
                return jnp.logical_and(j >= 0, cmax > SB_EXIT_LOG)

            def body(st):
                j, _, carry, acc = st
                carry, acc = tile(k_ref[rows(j), lane_tile(hh)], vt_ref[j, head_rows(hh), :], u, carry, acc)
                return j - 1, jnp.max(carry), carry, acc

            st = (i - 2, jnp.max(carries[hh]), carries[hh], out_scr[head_rows(hh), :])
            _, cmax, carry, acc = lax.while_loop(cond, body, st)

            def meta(acc):
                return tile(km_ref[:, lane_tile(hh)], vmt_ref[head_rows(hh), :], um_ref[...], carry, acc)[1]

            out_scr[head_rows(hh), :] = lax.cond(cmax > SB_EXIT_LOG, meta, lambda a: a, acc)

    def write_out(i):
        o_ref[rows(i), :] = out_scr[...].T.astype(o_ref.dtype)

    zi_scr[...] = jnp.zeros(zi_scr.shape, f32)
    out_scr[...] = jnp.zeros(out_scr.shape, f32)
    zero_row = jnp.zeros((1, TS), f32)

    def body(i, tots):
        ib = jnp.maximum(i - 1, 0)
        write_out(jnp.maximum(i - 2, 0))
        carries = stage_b(ib, tots)
        tots = stage_a(jnp.minimum(i, nq - 1))
        worst = functools.reduce(jnp.maximum, carries)

        @pl.when(jnp.logical_and(i >= 1, jnp.max(worst) > SB_EXIT_LOG))
        def _():
            further_tiles(ib, carries)

        return tots

    lax.fori_loop(0, nq + 1, body, (zero_row,) * (2 * SB_HP))
    write_out(nq - 1)


def _sb_call(qst, ks, vst, ksm, vsmt, u, um):
    B, S, D = ks.shape
    width = SB_HP * SB_HEAD_DIM
    const = lambda b, h: (0, 0)
    slab = pl.BlockSpec((None, S, width), lambda b, h: (b, 0, h))
    tiles_t = pl.BlockSpec((None, S // TS, width, TS), lambda b, h: (b, 0, h, 0))
    return pl.pallas_call(
        _sb_kernel,
        grid=(B, SB_HEADS // SB_HP),
        in_specs=[tiles_t, slab, tiles_t,
                  pl.BlockSpec((N_META, width), lambda b, h: (0, h)),
                  pl.BlockSpec((width, N_META), lambda b, h: (h, 0)),
                  pl.BlockSpec((TS, TS), const), pl.BlockSpec((N_META, N_META), const)],
        out_specs=slab,
        out_shape=jax.ShapeDtypeStruct((B, S, D), jnp.bfloat16),
        scratch_shapes=[pltpu.VMEM((2 * SB_HP, TS, TS), jnp.float32),
                        pltpu.VMEM((width, TS), jnp.float32)],
        compiler_params=pltpu.CompilerParams(dimension_semantics=("arbitrary",) * 2,
                                             vmem_limit_bytes=VMEM_LIMIT),
        name="sb_attn",
    )(qst, ks, vst, ksm, vsmt, u, um)


def _sigmoid(x):
    return 1.0 / (1.0 + jnp.exp(-x))


def _final_kernel(x_ref, yd_ref, ys_ref, eg_ref, eb_ref, wgd_ref, wgs_ref, wmd_ref, wms_ref,
                  wbd_ref, wbs_ref, wo_ref, bg_ref, lg_ref, lb_ref, o_ref):
    for r in range(TM_FINAL // FINAL_CHUNK):
        rs = slice(r * FINAL_CHUNK, (r + 1) * FINAL_CHUNK)
        h = _layer_norm(x_ref[rs, :], eg_ref[...], eb_ref[...])
        hb = h.astype(jnp.bfloat16)

        def proj(w_ref):
            return jnp.dot(hb, w_ref[...], preferred_element_type=jnp.float32)

        g_d = proj(wgd_ref)
        u_d = (yd_ref[rs, :].astype(jnp.float32) * (g_d * _sigmoid(g_d))).astype(jnp.bfloat16)
        br_d = jnp.dot(u_d, wbd_ref[...], preferred_element_type=jnp.float32)
        merged = _sigmoid(proj(wmd_ref) + bg_ref[0:1, :]) * br_d
        g_s = proj(wgs_ref)
        u_s = (ys_ref[rs, :].astype(jnp.float32) * (g_s * _sigmoid(g_s))).astype(jnp.bfloat16)
        br_s = jnp.dot(u_s, wbs_ref[...], preferred_element_type=jnp.float32)
        merged = merged + _sigmoid(proj(wms_ref) + bg_ref[1:2, :]) * br_s
        y = jnp.dot(merged.astype(jnp.bfloat16), wo_ref[...], preferred_element_type=jnp.float32)
        o_ref[rs, :] = _layer_norm(DN_ALPHA * h + y, lg_ref[...], lb_ref[...]).astype(o_ref.dtype)


def _final_call(x2, yd2, ys2, eg, eb, w_all, wbd, wbs, wo, bg, lg, lb):
    R, D = x2.shape
    row = pl.BlockSpec((TM_FINAL, D), lambda i: (i, 0))
    const = lambda i: (0, 0)
    vec = pl.BlockSpec((1, D), const)
    weight = pl.BlockSpec((D, D), const, pipeline_mode=pl.Buffered(1))
    return pl.pallas_call(
        _final_kernel,
        grid=(R // TM_FINAL,),
        in_specs=[row, row, row, vec, vec]
        + [_w_block(c, 1) for c in (COL_GD, COL_GS, COL_MD, COL_MS)]
        + [weight, weight, weight, pl.BlockSpec((2, D), const), vec, vec],
        out_specs=row,
        out_shape=jax.ShapeDtypeStruct((R, D), jnp.float32),
        compiler_params=pltpu.CompilerParams(dimension_semantics=("arbitrary",),
                                             vmem_limit_bytes=VMEM_LIMIT),
        name="final",
    )(x2, yd2, ys2, eg, eb, w_all, w_all, w_all, w_all, wbd, wbs, wo, bg, lg, lb)


def _neg_upper_ones(n):
    r = jnp.arange(n)
    return jnp.where(r[None, :] >= r[:, None], -1.0, 0.0).astype(jnp.bfloat16)


def kernel(x, meta_tokens, emb_ln_g, emb_ln_b, w_in, b_gate, diff_lambda, diff_subln_g,
           w_br_diff, w_br_sb, w_out, ln_g, ln_b):
    B, S, D = x.shape
    bf16 = jnp.bfloat16
    qscale = DIFF_HEAD_DIM ** -0.5
    col_scale = np.ones((N_COLS * D,), np.float32)
    col_scale[COL_QD * D:(COL_QD + 1) * D] = qscale * LOG2E
    col_scale[COL_QS * D:(COL_QS + 1) * D] = qscale
    w_all = (w_in[0] * col_scale).astype(bf16)
    eg, eb = emb_ln_g.reshape(1, D), emb_ln_b.reshape(1, D)

    groups = 2 * DIFF_HEADS
    gsel = (np.arange(D)[:, None] // DIFF_HEAD_DIM == np.arange(LANES)[None, :]).astype(np.float32)
    kd, ks, qdt, qst, vdt, vst, kn2 = _proj_call(x, eg, eb, w_all, jnp.asarray(gsel, bf16))
    kd_meta, ks_meta, vdt_meta, vst_meta = _meta_proj_call(meta_tokens, eg, eb, w_all)

    kpre = jnp.sqrt(lax.cummax(kn2[:, :, 0, :groups], axis=1))
    kpre = kpre.transpose(0, 2, 1).reshape(-1)

    slopes = 2.0 ** (-8.0 * jnp.arange(1, DIFF_HEADS + 1, dtype=jnp.float32) / DIFF_HEADS)

    yd = _diff_call(slopes, kpre, diff_lambda[0], qdt, kd, vdt, kd_meta, vdt_meta,
                    diff_subln_g[0].reshape(DIFF_V_DIM, 1))
    ys = _sb_call(qst, ks, vst, ks_meta, vst_meta, _neg_upper_ones(TS), _neg_upper_ones(N_META))

    out = _final_call(x.reshape(B * S, D), yd.reshape(B * S, D), ys.reshape(B * S, D), eg, eb,
                      w_all, w_br_diff[0].astype(bf16), w_br_sb[0].astype(bf16), w_out[0].astype(bf16),
                      b_gate[0], ln_g[0].reshape(1, D), ln_b[0].reshape(1, D))
    return out.reshape(B, S, D)
```
